```python
import math
import jax, jax.numpy as jnp
from jax import lax
import numpy as np

D_MODEL = 2048
BATCH = 4
SEQ = 2048
DEPTH = 4
DEC_BATCH = 128
DEC_SEQ = 1
PAST_LEN = 16384
PAGE_SIZE = 128

N_MEM = 256
BRANCH_W = D_MODEL // 2
N_BRANCH = 3
RW_HEAD = 64
RW_HEADS = BRANCH_W // RW_HEAD
RW_DECAY_LORA = 64
RW_AAA_LORA = 64
RW_GATE_LORA = 160
RW_PROJ = 3 * BRANCH_W + RW_DECAY_LORA + RW_AAA_LORA + RW_GATE_LORA
RW_GN_EPS = 64e-5
SSD_HEAD = 64
SSD_HEADS = BRANCH_W // SSD_HEAD
SSD_GROUPS = 2
SSD_HPG = SSD_HEADS // SSD_GROUPS
SSD_STATE = 128
SSD_CONV = 4
SSD_CONV_DIM = BRANCH_W + 2 * SSD_GROUPS * SSD_STATE
SSD_PROJ = BRANCH_W + SSD_CONV_DIM + SSD_HEADS
SSD_CHUNK = 64
GLA_HEADS = 4
GLA_DK = BRANCH_W // 2
GLA_DV = BRANCH_W
GLA_HK = GLA_DK // GLA_HEADS
GLA_HV = GLA_DV // GLA_HEADS
GLA_LORA = 16
GLA_TAU = 16.0
GLA_PROJ = 2 * GLA_DK + 2 * GLA_DV + GLA_LORA
GLA_CHUNK = 64
N_IN = RW_PROJ + SSD_PROJ + GLA_PROJ + N_BRANCH * D_MODEL
X_HEADS = 4
X_HEAD = D_MODEL // X_HEADS
D_FF = 4 * D_MODEL
NORM_EPS = 1e-6
GROUP_EPS = 1e-5

kernel_name = 'hybrid_rwkv7_ssd_gla_gated_step'


def rms_norm(x, g, eps=NORM_EPS):
    xf = x.astype(jnp.float32)
    y = xf * lax.rsqrt(jnp.mean(xf * xf, axis=-1, keepdims=True) + eps)
    return (y * g.astype(jnp.float32)).astype(x.dtype)


def to_chunks(z, size):
    b, t = z.shape[:2]
    pad = (-t) % size
    z = jnp.pad(z, [(0, 0), (0, pad)] + [(0, 0)] * (z.ndim - 2))
    return jnp.moveaxis(z.reshape(b, (t + pad) // size, size, *z.shape[2:]), 1, 0)


def from_chunks(y, t):
    nc, b, size = y.shape[:3]
    return jnp.moveaxis(y, 0, 1).reshape(b, nc * size, *y.shape[3:])[:, :t]


def rwkv7_branch(p, shift_prev, s0, mu, w0, w2, a0, a2, g2, k_k, k_a, r_k, ln_w, ln_b):
    f32 = jnp.float32
    b, t, _ = p.shape
    c = BRANCH_W
    prev = jnp.concatenate([shift_prev[:, None, :].astype(p.dtype), p[:, :-1]], axis=1)
    ps = p + (prev - p) * mu
    r, k, v, wd, ad, gd = jnp.split(ps, [c, 2 * c, 3 * c, 3 * c + RW_DECAY_LORA, 3 * c + RW_DECAY_LORA + RW_AAA_LORA], axis=-1)
    w = -jax.nn.softplus(-(w0 + jnp.tanh(wd) @ w2).astype(f32)) - 0.5
    decay = jnp.exp(-jnp.exp(w))
    a = jax.nn.sigmoid((a0 + ad @ a2).astype(f32))
    g = (jax.nn.sigmoid(gd) @ g2).astype(f32)
    heads = lambda z: z.astype(f32).reshape(b, t, RW_HEADS, RW_HEAD)
    kk = heads(k * k_k)
    kk = kk / jnp.maximum(jnp.sqrt(jnp.sum(kk * kk, axis=-1, keepdims=True)), 1e-12)
    k = k.astype(f32) * (1.0 + (a - 1.0) * k_a)
    r_h, k_h, v_h, a_h = heads(r), heads(k), heads(v), heads(a)

    def step(s, inp):
        r_t, w_t, k_t, v_t, da_t, db_t = inp
        sa = jnp.einsum('bhij,bhj->bhi', s, da_t)
        s = s * w_t[:, :, None, :] + sa[..., None] * db_t[:, :, None, :] + v_t[..., None] * k_t[:, :, None, :]
        return s, jnp.einsum('bhij,bhj->bhi', s, r_t)

    seq_first = lambda z: jnp.moveaxis(z, 1, 0)
    xs = (seq_first(r_h), seq_first(heads(decay)), seq_first(k_h), seq_first(v_h), seq_first(-kk), seq_first(kk * a_h))
    s, y = lax.scan(step, s0.astype(f32), xs)
    y = jnp.moveaxis(y, 0, 1)
    mean = jnp.mean(y, axis=-1, keepdims=True)
    var = jnp.mean(jnp.square(y - mean), axis=-1, keepdims=True)
    y = ((y - mean) * lax.rsqrt(var + RW_GN_EPS)).reshape(b, t, c) * ln_w + ln_b
    bonus = jnp.sum(r_h * k_h * r_k, axis=-1, keepdims=True) * v_h
    y = (y + bonus.reshape(b, t, c)) * g
    return y.astype(p.dtype), s, p[:, -1]


def ssd_chunked(xdt, da, bm, cm, s0):
    t = xdt.shape[1]
    size = min(SSD_CHUNK, t)
    mask = jnp.tril(jnp.ones((size, size), dtype=bool))[None, :, :, None, None]

    def body(s, inp):
        xc, ac, bc, cc = inp
        cum = jnp.cumsum(ac, axis=1)
        seg = jnp.exp(jnp.where(mask, cum[:, :, None] - cum[:, None], -jnp.inf))
        cb = jnp.einsum('blgn,bsgn->blsg', cc, bc)
        y = jnp.einsum('blsg,blsgh,bsghp->blghp', cb, seg, xc)
        y = y + jnp.einsum('blgn,bghpn,blgh->blghp', cc, s, jnp.exp(cum))
        last = cum[:, -1]
        s = s * jnp.exp(last)[..., None, None] + jnp.einsum('bsgn,bsgh,bsghp->bghpn', bc, jnp.exp(last[:, None] - cum), xc)
        return s, y

    s, ys = lax.scan(body, s0, (to_chunks(xdt, size), to_chunks(da, size), to_chunks(bm, size), to_chunks(cm, size)))
    return from_chunks(ys, t), s


def ssd_branch(p, conv_prev, s0, conv_w, conv_b, dt_bias, a_log, d_skip, norm_w):
    f32 = jnp.float32
    b, t, _ = p.shape
    z, xbc, dt = jnp.split(p, [BRANCH_W, BRANCH_W + SSD_CONV_DIM], axis=-1)
    full = jnp.concatenate([conv_prev.astype(p.dtype), xbc], axis=1)
    conv = lax.conv_general_dilated(full, conv_w[:, None, :].astype(p.dtype), window_strides=(1,), padding='VALID',
                                    dimension_numbers=('NWC', 'WIO', 'NWC'), feature_group_count=SSD_CONV_DIM)
    xbc = jax.nn.silu((conv + conv_b).astype(f32))
    xs, bm, cm = jnp.split(xbc, [BRANCH_W, BRANCH_W + SSD_GROUPS * SSD_STATE], axis=-1)
    xs = xs.reshape(b, t, SSD_GROUPS, SSD_HPG, SSD_HEAD)
    bm = bm.reshape(b, t, SSD_GROUPS, SSD_STATE)
    cm = cm.reshape(b, t, SSD_GROUPS, SSD_STATE)
    dt = jax.nn.softplus(dt.astype(f32) + dt_bias).reshape(b, t, SSD_GROUPS, SSD_HPG)
    da = dt * (-jnp.exp(a_log.astype(f32))).reshape(SSD_GROUPS, SSD_HPG)
    s0 = s0.astype(f32).reshape(b, SSD_GROUPS, SSD_HPG, SSD_HEAD, SSD_STATE)
    y, s = ssd_chunked(xs * dt[..., None], da, bm, cm, s0)
    y = y + xs * d_skip.astype(f32).reshape(SSD_GROUPS, SSD_HPG)[:, :, None]
    y = y.reshape(b, t, BRANCH_W) * jax.nn.silu(z.astype(f32))
    y = y.reshape(b, t, SSD_GROUPS, BRANCH_W // SSD_GROUPS)
    y = y * lax.rsqrt(jnp.mean(y * y, axis=-1, keepdims=True) + GROUP_EPS)
    y = y.reshape(b, t, BRANCH_W) * norm_w
    return y.astype(p.dtype), s.reshape(b, SSD_HEADS, SSD_HEAD, SSD_STATE), full[:, -(SSD_CONV - 1):]


def gla_chunked(q, k, v, lg, s0):
    t = q.shape[1]
    size = min(GLA_CHUNK, t)
    mask = jnp.tril(jnp.ones((size, size), dtype=bool))[None, :, :, None, None]

    def body(s, inp):
        qc, kc, vc, gc = inp
        cum = jnp.cumsum(gc, axis=1)
        dec = jnp.exp(jnp.where(mask, cum[:, :, None] - cum[:, None], -jnp.inf))
        att = jnp.einsum('blhd,bshd,blshd->blsh', qc, kc, dec)
        o = jnp.einsum('blsh,bshv->blhv', att, vc) + jnp.einsum('blhd,bhdv->blhv', qc * jnp.exp(cum), s)
        last = cum[:, -1]
        s = s * jnp.exp(last)[..., None] + jnp.einsum('bshd,bshv->bhdv', kc * jnp.exp(last[:, None] - cum), vc)
        return s, o

    s, os_ = lax.scan(body, s0, (to_chunks(q, size), to_chunks(k, size), to_chunks(v, size), to_chunks(lg, size)))
    return from_chunks(os_, t), s


def gla_branch(p, s0, alpha_up, alpha_b, norm_w):
    f32 = jnp.float32
    b, t, _ = p.shape
    q, k, v, r, ad = jnp.split(p, [GLA_DK, 2 * GLA_DK, 2 * GLA_DK + GLA_DV, 2 * GLA_DK + 2 * GLA_DV], axis=-1)
    lg = jax.nn.log_sigmoid((ad @ alpha_up + alpha_b).astype(f32)) / GLA_TAU
    hk = lambda z: z.astype(f32).reshape(b, t, GLA_HEADS, GLA_HK)
    vh = v.astype(f32).reshape(b, t, GLA_HEADS, GLA_HV)
    o, s = gla_chunked(hk(q) * (GLA_HK ** -0.5), hk(k), vh, hk(lg), s0.astype(f32))
    o = o * lax.rsqrt(jnp.mean(o * o, axis=-1, keepdims=True) + GROUP_EPS) * norm_w
    o = o.reshape(b, t, GLA_DV) * jax.nn.silu(r.astype(f32))
    return o.astype(p.dtype), s


def memory_kv(mem, g, wk, wv):
    b = mem.shape[0]
    m = rms_norm(mem, g)
    return (m @ wk).reshape(b, N_MEM, X_HEADS, X_HEAD), (m @ wv).reshape(b, N_MEM, X_HEADS, X_HEAD)


def cross_attention(h, mem_k, mem_v, wq, wo):
    b, t, _ = h.shape
    q = (h @ wq).reshape(b, t, X_HEADS, X_HEAD)
    s = jnp.einsum('bthd,bmhd->bhtm', q, mem_k).astype(jnp.float32) * (X_HEAD ** -0.5)
    attn = jax.nn.softmax(s, axis=-1).astype(h.dtype)
    o = jnp.einsum('bhtm,bmhd->bthd', attn, mem_v).reshape(b, t, D_MODEL)
    return o @ wo


def trunk_layer(x, rw_s, rw_shift, ssd_s, ssd_buf, gla_s, mem_k, mem_v,
                n_mix_pre, n_mix_post, n_x_pre, n_x_post, n_ffn_pre, n_ffn_post,
                w_in, rw_mu, rw_w0, rw_w2, rw_a0, rw_a2, rw_g2, rw_kk, rw_ka, rw_rk, rw_ln_w, rw_ln_b,
                ssd_conv_w, ssd_conv_b, ssd_dt_bias, ssd_a_log, ssd_d, ssd_norm_w,
                gla_alpha_up, gla_alpha_b, gla_norm_w, w_branch, w_out, x_wq, x_wo, ffn_up, ffn_down):
    b, t, _ = x.shape
    proj = rms_norm(x, n_mix_pre) @ w_in
    o1 = RW_PROJ
    o2 = o1 + SSD_PROJ
    o3 = o2 + GLA_PROJ
    p_rw, p_ssd, p_gla, p_gate = jnp.split(proj, [o1, o2, o3], axis=-1)
    y_a, rw_s, rw_shift = rwkv7_branch(p_rw, rw_shift, rw_s, rw_mu, rw_w0, rw_w2, rw_a0, rw_a2, rw_g2,
                                       rw_kk, rw_ka, rw_rk, rw_ln_w, rw_ln_b)
    y_b, ssd_s, ssd_buf = ssd_branch(p_ssd, ssd_buf, ssd_s, ssd_conv_w, ssd_conv_b, ssd_dt_bias, ssd_a_log, ssd_d, ssd_norm_w)
    y_c, gla_s = gla_branch(p_gla, gla_s, gla_alpha_up, gla_alpha_b, gla_norm_w)
    z = jnp.einsum('nbtw,nwd->nbtd', jnp.stack([y_a, y_b, y_c]), w_branch)
    gate = jax.nn.sigmoid(p_gate.astype(jnp.float32)).reshape(b, t, N_BRANCH, D_MODEL).astype(x.dtype)
    mix = jnp.einsum('nbtd,btnd->btd', z, gate) @ w_out
    x = x + rms_norm(mix, n_mix_post)
    x = x + rms_norm(cross_attention(rms_norm(x, n_x_pre), mem_k, mem_v, x_wq, x_wo), n_x_post)
    hf = rms_norm(x, n_ffn_pre) @ ffn_up
    x = x + rms_norm(jnp.square(jax.nn.relu(hf)) @ ffn_down, n_ffn_post)
    dt = x.dtype
    return x, (rw_s.astype(dt), rw_shift.astype(dt), ssd_s.astype(dt), ssd_buf.astype(dt), gla_s.astype(dt))


def setup_inputs(seed: int = 0) -> dict:
    key = jax.random.key(seed)
    keys = iter(jax.random.split(key, 64))
    f32 = jnp.float32

    def nrm(shape, scale):
        return jax.random.normal(next(keys), shape, f32) * scale

    def unif(shape, lo, hi):
        return jax.random.uniform(next(keys), shape, f32, lo, hi)

    def gain(shape, s=0.02):
        return 1.0 + nrm(shape, s)

    L, D = DEPTH, D_MODEL
    dt0 = jnp.exp(unif((L, SSD_HEADS), math.log(1e-3), math.log(1e-1)))
    return {
        'x_prompt': nrm((BATCH, SEQ, D), 1.0),
        'x_sample': nrm((DEC_BATCH, DEC_SEQ, D), 1.0),
        'state_rwkv': nrm((L, DEC_BATCH, RW_HEADS, RW_HEAD, RW_HEAD), 0.1),
        'state_rwkv_shift': nrm((L, DEC_BATCH, RW_PROJ), 1.0),
        'state_ssd': nrm((L, DEC_BATCH, SSD_HEADS, SSD_HEAD, SSD_STATE), 0.1),
        'state_ssd_conv': nrm((L, DEC_BATCH, SSD_CONV - 1, SSD_CONV_DIM), 1.0),
        'state_gla': nrm((L, DEC_BATCH, GLA_HEADS, GLA_HK, GLA_HV), 0.1),
        'cache_mem_k': nrm((L, DEC_BATCH, N_MEM, X_HEADS, X_HEAD), 1.0),
        'cache_mem_v': nrm((L, DEC_BATCH, N_MEM, X_HEADS, X_HEAD), 1.0),
        'mem_prompt': nrm((BATCH, N_MEM, D), 1.0),
        'norm_mix_pre': gain((L, D)),
        'norm_mix_post': gain((L, D)),
        'norm_x_pre': gain((L, D)),
        'norm_x_post': gain((L, D)),
        'norm_ffn_pre': gain((L, D)),
        'norm_ffn_post': gain((L, D)),
        'w_in': nrm((L, D, N_IN), D ** -0.5),
        'rw_mu': unif((L, RW_PROJ), 0.0, 1.0),
        'rw_w0': unif((L, BRANCH_W), -6.0, 1.0),
        'rw_w2': nrm((L, RW_DECAY_LORA, BRANCH_W), 0.1),
        'rw_a0': nrm((L, BRANCH_W), 0.1),
        'rw_a2': nrm((L, RW_AAA_LORA, BRANCH_W), 0.1),
        'rw_g2': nrm((L, RW_GATE_LORA, BRANCH_W), RW_GATE_LORA ** -0.5),
        'rw_kk': 0.85 + nrm((L, BRANCH_W), 0.02),
        'rw_ka': gain((L, BRANCH_W)),
        'rw_rk': nrm((L, RW_HEADS, RW_HEAD), 0.1),
        'rw_ln_w': gain((L, BRANCH_W)),
        'rw_ln_b': nrm((L, BRANCH_W), 0.01),
        'ssd_conv_w': nrm((L, SSD_CONV, SSD_CONV_DIM), SSD_CONV ** -0.5),
        'ssd_conv_b': nrm((L, SSD_CONV_DIM), 0.01),
        'ssd_dt_bias': dt0 + jnp.log(-jnp.expm1(-dt0)),
        'ssd_a_log': jnp.log(unif((L, SSD_HEADS), 1.0, 16.0)),
        'ssd_d': gain((L, SSD_HEADS), 0.1),
        'ssd_norm_w': gain((L, BRANCH_W)),
        'gla_alpha_up': nrm((L, GLA_LORA, GLA_DK), GLA_LORA ** -0.5),
        'gla_alpha_b': nrm((L, GLA_DK), 0.1),
        'gla_norm_w': gain((L, GLA_HV)),
        'w_branch': nrm((L, N_BRANCH, BRANCH_W, D), BRANCH_W ** -0.5),
        'w_out': nrm((L, D, D), D ** -0.5),
        'x_mem_norm': gain((L, D)),
        'x_wq': nrm((L, D, D), D ** -0.5),
        'x_wk': nrm((L, D, D), D ** -0.5),
        'x_wv': nrm((L, D, D), D ** -0.5),
        'x_wo': nrm((L, D, D), D ** -0.5),
        'ffn_up': nrm((L, D, D_FF), D ** -0.5),
        'ffn_down': nrm((L, D_FF, D), D_FF ** -0.5),
    }


def reference(x_prompt, x_sample, state_rwkv, state_rwkv_shift, state_ssd, state_ssd_conv, state_gla,
              cache_mem_k, cache_mem_v, mem_prompt,
              norm_mix_pre, norm_mix_post, norm_x_pre, norm_x_post, norm_ffn_pre, norm_ffn_post,
              w_in, rw_mu, rw_w0, rw_w2, rw_a0, rw_a2, rw_g2, rw_kk, rw_ka, rw_rk, rw_ln_w, rw_ln_b,
              ssd_conv_w, ssd_conv_b, ssd_dt_bias, ssd_a_log, ssd_d, ssd_norm_w,
              gla_alpha_up, gla_alpha_b, gla_norm_w, w_branch, w_out,
              x_mem_norm, x_wq, x_wk, x_wv, x_wo, ffn_up, ffn_down):
    layer_w = (norm_mix_pre, norm_mix_post, norm_x_pre, norm_x_post, norm_ffn_pre, norm_ffn_post,
               w_in, rw_mu, rw_w0, rw_w2, rw_a0, rw_a2, rw_g2, rw_kk, rw_ka, rw_rk, rw_ln_w, rw_ln_b,
               ssd_conv_w, ssd_conv_b, ssd_dt_bias, ssd_a_log, ssd_d, ssd_norm_w,
               gla_alpha_up, gla_alpha_b, gla_norm_w, w_branch, w_out, x_wq, x_wo, ffn_up, ffn_down)
    b, dt = x_prompt.shape[0], x_prompt.dtype
    xp, xs = x_prompt, x_sample
    p_acc = [[] for _ in range(7)]
    s_acc = [[] for _ in range(5)]
    for l in range(DEPTH):
        lw = [w[l] for w in layer_w]
        mk, mv = memory_kv(mem_prompt, x_mem_norm[l], x_wk[l], x_wv[l])
        xp, new_p = trunk_layer(xp,
                                jnp.zeros((b, RW_HEADS, RW_HEAD, RW_HEAD), dt),
                                jnp.zeros((b, RW_PROJ), dt),
                                jnp.zeros((b, SSD_HEADS, SSD_HEAD, SSD_STATE), dt),
                                jnp.zeros((b, SSD_CONV - 1, SSD_CONV_DIM), dt),
                                jnp.zeros((b, GLA_HEADS, GLA_HK, GLA_HV), dt),
                                mk, mv, *lw)
        xs, new_s = trunk_layer(xs, state_rwkv[l], state_rwkv_shift[l], state_ssd[l], state_ssd_conv[l], state_gla[l],
                                cache_mem_k[l], cache_mem_v[l], *lw)
        for acc, val in zip(p_acc, new_p + (mk, mv)):
            acc.append(val)
        for acc, val in zip(s_acc, new_s):
            acc.append(val)
    p_rwkv, p_rwkv_shift, p_ssd, p_ssd_conv, p_gla, p_mem_k, p_mem_v = [jnp.stack(a) for a in p_acc]
    s_rwkv, s_rwkv_shift, s_ssd, s_ssd_conv, s_gla = [jnp.stack(a) for a in s_acc]
    return (xp, xs, p_rwkv, p_rwkv_shift, p_ssd, p_ssd_conv, p_gla, p_mem_k, p_mem_v,
            s_rwkv, s_rwkv_shift, s_ssd, s_ssd_conv, s_gla)
```

```python
import functools

import jax
import jax.numpy as jnp
from jax import lax
from jax.experimental import pallas as pl
from jax.experimental.pallas import tpu as pltpu

F32 = jnp.float32
BF16 = jnp.bfloat16

D_MODEL = 2048
BRANCH_W = 1024
N_BRANCH = 3
RW_HEADS = 16
RW_HEAD = 64
RW_PROJ = 3360
RW_PAD = 3456
RW_GN_EPS = 64e-5
SSD_HEADS = 16
SSD_HEAD = 64
SSD_GROUPS = 2
SSD_STATE = 128
SSD_CONV = 4
SSD_CONV_DIM = 1536
SSD_PROJ = 2576
SSD_PAD = 2688
GLA_HEADS = 4
GLA_DK = 512
GLA_DV = 1024
GLA_HK = 128
GLA_HV = 256
GLA_LORA = 16
GLA_TAU = 16.0
GLA_PROJ = 3088
GLA_PAD = 3200
N_MEM = 256
X_HEADS = 4
X_HEAD = 512
D_FF = 8192
NORM_EPS = 1e-6
GROUP_EPS = 1e-5
CHUNK = 64
SUB = 16
LANES = 128
VMEM_LIMIT = 56 * 1024 * 1024

NN = ((1,), (0,))
NT = ((1,), (1,))
TN = ((0,), (0,))


def _cparams(sem):
    return pltpu.CompilerParams(dimension_semantics=sem, vmem_limit_bytes=VMEM_LIMIT)


def _dg(a, b, dims=NN):
    return lax.dot_general(a, b, (dims, ((), ())), preferred_element_type=F32)


def _dgb(a, b, dims=NN):
    return _dg(a.astype(BF16), b.astype(BF16), dims)


def _hl(x):
    h = x.astype(BF16)
    return h, (x - h.astype(F32)).astype(BF16)


def _dot_hi(a, b, dims=NN):
    ah, al = _hl(a)
    bh, bl = _hl(b)
    return _dg(ah, bh, dims) + (_dg(ah, bl, dims) + _dg(al, bh, dims))


def _split3(x):
    h = x.astype(BF16)
    r = x - h.astype(F32)
    m = r.astype(BF16)
    return h, m, (r - m.astype(F32)).astype(BF16)


def _dot_sel_l(sel, x):
    h, m, l = _split3(x)
    return _dg(sel, h) + (_dg(sel, m) + _dg(sel, l))


def _dot_sel_r(x, sel):
    h, m, l = _split3(x)
    return _dg(h, sel) + (_dg(m, sel) + _dg(l, sel))


def _softplus(x):
    return jnp.maximum(x, 0.0) + jnp.log1p(jnp.exp(-jnp.abs(x)))


def _sigmoid(x):
    return 1.0 / (1.0 + jnp.exp(-x))


def _silu(x):
    return x * _sigmoid(x)


def _tri_incl(n):
    r = lax.broadcasted_iota(jnp.int32, (n, n), 0)
    c = lax.broadcasted_iota(jnp.int32, (n, n), 1)
    return jnp.where(c <= r, 1.0, 0.0).astype(BF16)


def _pair_masks(rows):
    lane = lax.broadcasted_iota(jnp.int32, (rows, LANES), 1)
    row = lax.broadcasted_iota(jnp.int32, (rows, LANES), 0)
    m0 = lane < 64
    col = jnp.bitwise_and(lane, 63)
    return m0, row, col


def _headsum_pair(x, m0):
    s0 = jnp.sum(jnp.where(m0, x, 0.0), axis=-1, keepdims=True)
    s1 = jnp.sum(jnp.where(m0, 0.0, x), axis=-1, keepdims=True)
    return jnp.where(m0, s0, s1)


def _bd(x, m0):
    return jnp.concatenate([jnp.where(m0, x, 0.0), jnp.where(m0, 0.0, x)], axis=0)


def _norm_kernel(x_ref, g_ref, o_ref):
    x = x_ref[...]
    y = x * lax.rsqrt(jnp.mean(x * x, axis=-1, keepdims=True) + NORM_EPS)
    o_ref[...] = (y * g_ref[...]).astype(o_ref.dtype)


def _norm_rows(x, g, tm):
    m, d = x.shape
    return pl.pallas_call(
        _norm_kernel,
        grid=(m // tm,),
        in_specs=[pl.BlockSpec((tm, d), lambda i: (i, 0)), pl.BlockSpec((1, d), lambda i: (0, 0))],
        out_specs=pl.BlockSpec((tm, d), lambda i: (i, 0)),
        out_shape=jax.ShapeDtypeStruct((m, d), BF16),
        compiler_params=_cparams(("parallel",)),
    )(x, g.reshape(1, d))


def _mm_kernel(a_ref, w_ref, o_ref, *scratch, nk, act):
    part = _dgb(a_ref[...], w_ref[...])

    def finish(acc):
        if act == "relu2":
            acc = jnp.square(jnp.maximum(acc, 0.0))
        o_ref[...] = acc.astype(o_ref.dtype)

    if nk == 1:
        finish(part)
        return
    acc_ref, = scratch
    k = pl.program_id(2)

    @pl.when(k == 0)
    def _():
        acc_ref[...] = part

    @pl.when(k > 0)
    def _():
        acc_ref[...] += part

    @pl.when(k == nk - 1)
    def _():
        finish(acc_ref[...])


def _mm_resnorm_kernel(a_ref, w_ref, res_ref, gp_ref, gn_ref, x_ref, xn_ref, *scratch, nk):
    part = _dgb(a_ref[...], w_ref[...])

    def finish(acc):
        y = acc * lax.rsqrt(jnp.mean(acc * acc, axis=-1, keepdims=True) + NORM_EPS) * gp_ref[...]
        x = res_ref[...] + y
        x_ref[...] = x
        xn = x * lax.rsqrt(jnp.mean(x * x, axis=-1, keepdims=True) + NORM_EPS) * gn_ref[...]
        xn_ref[...] = xn.astype(xn_ref.dtype)

    if nk == 1:
        finish(part)
        return
    acc_ref, = scratch
    k = pl.program_id(1)

    @pl.when(k == 0)
    def _():
        acc_ref[...] = part

    @pl.when(k > 0)
    def _():
        acc_ref[...] += part

    @pl.when(k == nk - 1)
    def _():
        finish(acc_ref[...])


def _pick(n, cands):
    for c in cands:
        if n % c == 0:
            return c
    return n


def _row_tile(m, cands):
    return _pick(m, cands)


def _mm(a, w, out_dtype=F32, act=None):
    m, kd = a.shape
    n = w.shape[1]
    tm = _row_tile(m, (1040, 1024, 512, 256, 128, 64, 8))
    tn = _pick(n, (1024, 1152, 896, 640, 512, 384, 256, 128))
    tk = _pick(kd, (2048, 1024, 512))
    nk = kd // tk
    scratch = [pltpu.VMEM((tm, tn), F32)] if nk > 1 else []
    return pl.pallas_call(
        functools.partial(_mm_kernel, nk=nk, act=act),
        grid=(m // tm, n // tn, nk),
        in_specs=[pl.BlockSpec((tm, tk), lambda i, j, k: (i, k)),
                  pl.BlockSpec((tk, tn), lambda i, j, k: (k, j))],
        out_specs=pl.BlockSpec((tm, tn), lambda i, j, k: (i, j)),
        out_shape=jax.ShapeDtypeStruct((m, n), out_dtype),
        scratch_shapes=scratch,
        compiler_params=_cparams(("parallel", "parallel", "arbitrary")),
    )(a, w)


def _mm_resnorm(a, w, res, g_post, g_next):
    m, kd = a.shape
    n = w.shape[1]
    tm = _row_tile(m, (520, 512, 256, 128, 64, 8))
    tk = _pick(kd, (1024, 512))
    nk = kd // tk
    scratch = [pltpu.VMEM((tm, n), F32)] if nk > 1 else []
    return pl.pallas_call(
        functools.partial(_mm_resnorm_kernel, nk=nk),
        grid=(m // tm, nk),
        in_specs=[pl.BlockSpec((tm, tk), lambda i, k: (i, k)),
                  pl.BlockSpec((tk, n), lambda i, k: (k, 0)),
                  pl.BlockSpec((tm, n), lambda i, k: (i, 0)),
                  pl.BlockSpec((1, n), lambda i, k: (0, 0)),
                  pl.BlockSpec((1, n), lambda i, k: (0, 0))],
        out_specs=[pl.BlockSpec((tm, n), lambda i, k: (i, 0)),
                   pl.BlockSpec((tm, n), lambda i, k: (i, 0))],
        out_shape=[jax.ShapeDtypeStruct((m, n), F32), jax.ShapeDtypeStruct((m, n), BF16)],
        scratch_shapes=scratch,
        compiler_params=_cparams(("parallel", "arbitrary")),
    )(a, w, res, g_post.reshape(1, n), g_next.reshape(1, n))


def _merge_kernel(y_ref, w_ref, g_ref, o_ref, acc_ref):
    n = pl.program_id(2)
    z = _dg(y_ref[0], w_ref[0]) * _sigmoid(g_ref[...])

    @pl.when(n == 0)
    def _():
        acc_ref[...] = z

    @pl.when(n > 0)
    def _():
        acc_ref[...] += z

    @pl.when(n == N_BRANCH - 1)
    def _():
        o_ref[...] = acc_ref[...].astype(o_ref.dtype)


def _merge(y3, wb, gate_logits):
    _, m, bw = y3.shape
    d = wb.shape[2]
    tm = _row_tile(m, (1040, 1024, 512, 256, 128, 64, 8))
    tn = 1024
    nj = d // tn
    return pl.pallas_call(
        _merge_kernel,
        grid=(m // tm, nj, N_BRANCH),
        in_specs=[pl.BlockSpec((1, tm, bw), lambda i, j, n: (n, i, 0)),
                  pl.BlockSpec((1, bw, tn), lambda i, j, n: (n, 0, j)),
                  pl.BlockSpec((tm, tn), lambda i, j, n: (i, n * nj + j))],
        out_specs=pl.BlockSpec((tm, tn), lambda i, j, n: (i, j)),
        out_shape=jax.ShapeDtypeStruct((m, d), BF16),
        scratch_shapes=[pltpu.VMEM((tm, tn), F32)],
        compiler_params=_cparams(("parallel", "parallel", "arbitrary")),
    )(y3, wb, gate_logits)


def _attn_p_kernel(q_ref, k_ref, v_ref, o_ref):
    s = _dgb(q_ref[...], k_ref[0], NT) * (X_HEAD ** -0.5)
    p = jnp.exp(s - jnp.max(s, axis=-1, keepdims=True))
    attn = p / jnp.sum(p, axis=-1, keepdims=True)
    o_ref[...] = _dgb(attn, v_ref[0]).astype(o_ref.dtype)


def _attn_prompt(q, mem_k, mem_v, nb, t):
    tq = _pick(t, (1024, 512, 256, 128, 64))
    nq = t // tq
    return pl.pallas_call(
        _attn_p_kernel,
        grid=(nb, X_HEADS, nq),
        in_specs=[pl.BlockSpec((tq, X_HEAD), lambda b, h, i: (b * nq + i, h)),
                  pl.BlockSpec((1, N_MEM, X_HEAD), lambda b, h, i: (b, 0, h)),
                  pl.BlockSpec((1, N_MEM, X_HEAD), lambda b, h, i: (b, 0, h))],
        out_specs=pl.BlockSpec((tq, X_HEAD), lambda b, h, i: (b * nq + i, h)),
        out_shape=jax.ShapeDtypeStruct((nb * t, D_MODEL), BF16),
        compiler_params=_cparams(("parallel", "parallel", "parallel")),
    )(q, mem_k, mem_v)


def _attn_s_kernel(q_ref, k_ref, v_ref, o_ref):
    q = q_ref[0].astype(F32)
    for h in range(X_HEADS):
        sl = slice(h * X_HEAD, (h + 1) * X_HEAD)
        s = jnp.sum(k_ref[0, 0, :, sl] * q[:, sl], axis=-1, keepdims=True) * (X_HEAD ** -0.5)
        p = jnp.exp(s - jnp.max(s, axis=0, keepdims=True))
        attn = p / jnp.sum(p, axis=0, keepdims=True)
        o_ref[0, :, sl] = jnp.sum(attn * v_ref[0, 0, :, sl], axis=0, keepdims=True).astype(o_ref.dtype)


def _attn_sample(q3, cache_k, cache_v, layer):
    nb = q3.shape[0]
    return pl.pallas_call(
        _attn_s_kernel,
        grid=(nb,),
        in_specs=[pl.BlockSpec((1, 1, D_MODEL), lambda b: (b, 0, 0)),
                  pl.BlockSpec((1, 1, N_MEM, D_MODEL), lambda b: (layer, b, 0, 0)),
                  pl.BlockSpec((1, 1, N_MEM, D_MODEL), lambda b: (layer, b, 0, 0))],
        out_specs=pl.BlockSpec((1, 1, D_MODEL), lambda b: (b, 0, 0)),
        out_shape=jax.ShapeDtypeStruct((nb, 1, D_MODEL), BF16),
        compiler_params=_cparams(("parallel",)),
    )(q3, cache_k, cache_v)


def _rwkv_tokens(p, prev, mu, w0, w2, a0, a2, g2, k_k, k_a):
    c = BRANCH_W
    ps = p + (prev - p) * mu
    r = ps[:, 0:c]
    k = ps[:, c:2 * c]
    v = ps[:, 2 * c:3 * c]
    slab = ps[:, 3 * c:3 * c + 128]
    gslab = ps[:, 3 * c + 128:RW_PAD]
    wl = w0 + _dgb(jnp.tanh(slab), w2)
    lw = -jnp.exp(-_softplus(-wl) - 0.5)
    a = _sigmoid(a0 + _dgb(slab, a2))
    g = _dgb(_sigmoid(gslab), g2)
    kkraw = k * k_k
    k2 = k * (1.0 + (a - 1.0) * k_a)
    return r, k2, v, lw, a, g, kkraw


def _rwkv_p_kernel(p_ref, mu_ref, w0_ref, w2_ref, a0_ref, a2_ref, g2_ref, kk_ref, ka_ref, rk_ref,
                   lnw_ref, lnb_ref, y_ref, s_ref, prev_scr):
    cidx = pl.program_id(1)

    @pl.when(cidx == 0)
    def _():
        s_ref[...] = jnp.zeros_like(s_ref)
        prev_scr[...] = jnp.zeros_like(prev_scr)

    p = p_ref[...]
    n = p.shape[0]
    row1 = lax.broadcasted_iota(jnp.int32, (n, 1), 0)
    prev = jnp.where(row1 == 0, prev_scr[0:1, :], pltpu.roll(p, 1, 0))
    prev_scr[0:1, :] = p[n - 1:n, :]
    r, k2, v, lw, a, g, kkraw = _rwkv_tokens(p, prev, mu_ref[...], w0_ref[...], w2_ref[...], a0_ref[...],
                                             a2_ref[...], g2_ref[...], kk_ref[...], ka_ref[...])
    cum = _dot_sel_l(_tri_incl(n), lw)
    e_c = jnp.exp(cum)
    e_x = jnp.exp(cum - lw)
    e_n = jnp.exp(-cum)
    e_l = jnp.exp(cum[n - 1:n, :] - cum)
    rk = rk_ref[...]
    lnw = lnw_ref[...]
    lnb = lnb_ref[...]

    m0, row, col = _pair_masks(n)
    strict = col < row
    incl = col <= row
    eye = jnp.where(col == row, 1.0, 0.0)
    ms, _, _ = _pair_masks(RW_HEAD)

    for pi in range(RW_HEADS // 2):
        sl = slice(pi * LANES, (pi + 1) * LANES)
        kkp = kkraw[:, sl]
        kkn = kkp / jnp.maximum(jnp.sqrt(_headsum_pair(kkp * kkp, m0)), 1e-12)
        bv = kkn * a[:, sl]
        at = -kkn * e_x[:, sl]
        rt = r[:, sl] * e_c[:, sl]
        bt = bv * e_n[:, sl]
        kt = k2[:, sl] * e_n[:, sl]
        bb = bv * e_l[:, sl]
        kb = k2[:, sl] * e_l[:, sl]
        vv = v[:, sl]
        ar = jnp.concatenate([at, rt], axis=0)
        sab = _dot_hi(ar, _bd(bt, m0), NT)
        sak = _dot_hi(ar, _bd(kt, m0), NT)
        a_ab = jnp.where(strict, sab[:n], 0.0)
        a_ak = jnp.where(strict, sak[:n], 0.0)
        m_rb = jnp.where(incl, sab[n:], 0.0)
        m_rk = jnp.where(incl, sak[n:], 0.0)
        tinv = eye + a_ab
        x = a_ab
        steps = n.bit_length() - 2
        for _ in range(steps):
            x = _dot_hi(x, _bd(x, m0))
            tinv = tinv + _dot_hi(tinv, _bd(x, m0))
        akv = _dot_hi(a_ak, _bd(vv, m0))
        tw = _dot_hi(tinv, jnp.concatenate([_bd(at, m0), _bd(akv, m0)], axis=1))
        ahat = tw[:, :LANES]
        w2v = tw[:, LANES:]
        s0 = s_ref[0, pi]
        us = _dot_hi(jnp.concatenate([ahat, rt], axis=0), _bd(s0, ms), NT)
        u = us[:n] + w2v
        y = us[n:] + _dot_hi(jnp.concatenate([m_rb, m_rk], axis=1),
                             jnp.concatenate([_bd(u, m0), _bd(vv, m0)], axis=0))
        z = _dot_hi(jnp.concatenate([u, vv], axis=0), jnp.concatenate([bb, kb], axis=0), TN)
        s_ref[0, pi] = s0 * e_c[n - 1:n, sl] + jnp.where(ms, z[:RW_HEAD], z[RW_HEAD:])
        mean = _headsum_pair(y, m0) * (1.0 / RW_HEAD)
        d = y - mean
        var = _headsum_pair(d * d, m0) * (1.0 / RW_HEAD)
        yn = d * lax.rsqrt(var + RW_GN_EPS) * lnw[:, sl] + lnb[:, sl]
        bonus = _headsum_pair(r[:, sl] * k2[:, sl] * rk[:, sl], m0) * vv
        y_ref[:, sl] = ((yn + bonus) * g[:, sl]).astype(y_ref.dtype)


def _rwkv_weights(w2, a2, g2):
    lw = w2.shape[0]
    w2p = jnp.concatenate([w2, jnp.zeros((LANES - lw, BRANCH_W), F32)], axis=0).astype(BF16)
    a2p = jnp.concatenate([jnp.zeros((lw, BRANCH_W), F32), a2], axis=0).astype(BF16)
    g2p = jnp.pad(g2, ((0, RW_PAD - 3 * BRANCH_W - LANES - g2.shape[0]), (0, 0))).astype(BF16)
    return w2p, a2p, g2p


def _row(x):
    return x.reshape(1, -1)


def _const_spec(shape):
    nd = len(shape)
    return pl.BlockSpec(shape, lambda *_: (0,) * nd)


def _rwkv_prompt(proj, nb, t, wts):
    nc = t // CHUNK
    consts = wts
    in_specs = [pl.BlockSpec((CHUNK, RW_PAD), lambda b, c: (b * nc + c, 0))]
    in_specs += [_const_spec(x.shape) for x in consts]
    return pl.pallas_call(
        _rwkv_p_kernel,
        grid=(nb, nc),
        in_specs=in_specs,
        out_specs=[pl.BlockSpec((CHUNK, BRANCH_W), lambda b, c: (b * nc + c, 0)),
                   pl.BlockSpec((1, RW_HEADS // 2, RW_HEAD, LANES), lambda b, c: (b, 0, 0, 0))],
        out_shape=[jax.ShapeDtypeStruct((nb * t, BRANCH_W), BF16),
                   jax.ShapeDtypeStruct((nb, RW_HEADS // 2, RW_HEAD, LANES), F32)],
        scratch_shapes=[pltpu.VMEM((8, RW_PAD), F32)],
        compiler_params=_cparams(("parallel", "arbitrary")),
    )(proj, *consts)


def _unpack_rwkv_state(sp):
    nb = sp.shape[0]
    s = sp.reshape(nb, RW_HEADS // 2, RW_HEAD, 2, RW_HEAD)
    return jnp.transpose(s, (0, 1, 3, 2, 4)).reshape(nb, RW_HEADS, RW_HEAD, RW_HEAD)


def _rwkv_s_prep_kernel(p_ref, prev_ref, mu_ref, w0_ref, w2_ref, a0_ref, a2_ref, g2_ref, kk_ref, ka_ref, rk_ref,
                        o_ref):
    r, k2, v, lw, a, g, kkraw = _rwkv_tokens(p_ref[...], prev_ref[...], mu_ref[...], w0_ref[...], w2_ref[...],
                                             a0_ref[...], a2_ref[...], g2_ref[...], kk_ref[...], ka_ref[...])
    n = r.shape[0]
    m0, _, _ = _pair_masks(n)
    rk = rk_ref[...]
    o_ref[0] = r
    o_ref[1] = jnp.exp(lw)
    o_ref[2] = k2
    o_ref[3] = v
    o_ref[6] = g
    for pi in range(RW_HEADS // 2):
        sl = slice(pi * LANES, (pi + 1) * LANES)
        kkp = kkraw[:, sl]
        kkn = kkp / jnp.maximum(jnp.sqrt(_headsum_pair(kkp * kkp, m0)), 1e-12)
        o_ref[4, :, sl] = -kkn
        o_ref[5, :, sl] = kkn * a[:, sl]
        o_ref[7, :, sl] = _headsum_pair(r[:, sl] * k2[:, sl] * rk[:, sl], m0) * v[:, sl]


def _rwkv_s_kernel(vec_ref, s_ref, lnw_ref, lnb_ref, so_ref, y_ref):
    nbb = s_ref.shape[1]
    n = RW_HEAD
    eye = lax.broadcasted_iota(jnp.int32, (n, n), 0) == lax.broadcasted_iota(jnp.int32, (n, n), 1)

    def body(bi, carry):
        for h in range(RW_HEADS):
            hs = pl.ds(h, 1)
            s = s_ref[0, bi, h]
            r_ = vec_ref[0, bi, hs, :]
            w_ = vec_ref[1, bi, hs, :]
            k_ = vec_ref[2, bi, hs, :]
            v_ = vec_ref[3, bi, hs, :]
            a_ = vec_ref[4, bi, hs, :]
            b_ = vec_ref[5, bi, hs, :]
            g_ = vec_ref[6, bi, hs, :]
            bonus_ = vec_ref[7, bi, hs, :]
            sa = jnp.sum(s * a_, axis=-1, keepdims=True)
            vc = jnp.sum(jnp.where(eye, v_, 0.0), axis=-1, keepdims=True)
            sn = s * w_ + sa * b_ + vc * k_
            so_ref[bi, h] = sn
            yc = jnp.sum(sn * r_, axis=-1, keepdims=True)
            yr = jnp.sum(jnp.where(eye, yc, 0.0), axis=0, keepdims=True)
            mean = jnp.mean(yr, axis=-1, keepdims=True)
            d = yr - mean
            var = jnp.mean(d * d, axis=-1, keepdims=True)
            yn = d * lax.rsqrt(var + RW_GN_EPS) * lnw_ref[hs, :] + lnb_ref[hs, :]
            y_ref[bi, hs, :] = (yn + bonus_) * g_
        return carry

    lax.fori_loop(0, nbb, body, 0)


def _rwkv_sample(proj, row0_blk, shift_prev, state, layer, wts, lnw, lnb):
    nb = shift_prev.shape[0]
    consts = wts
    vec = pl.pallas_call(
        _rwkv_s_prep_kernel,
        grid=(1,),
        in_specs=[pl.BlockSpec((nb, RW_PAD), lambda i: (row0_blk, 0)),
                  pl.BlockSpec((nb, RW_PAD), lambda i: (0, 0))] + [_const_spec(x.shape) for x in consts],
        out_specs=pl.BlockSpec((8, nb, BRANCH_W), lambda i: (0, 0, 0)),
        out_shape=jax.ShapeDtypeStruct((8, nb, BRANCH_W), F32),
        compiler_params=_cparams(("arbitrary",)),
    )(proj, shift_prev, *consts)
    vec = vec.reshape(8, nb, RW_HEADS, RW_HEAD)
    bb = 8
    s_new, y = pl.pallas_call(
        _rwkv_s_kernel,
        grid=(nb // bb,),
        in_specs=[pl.BlockSpec((8, bb, RW_HEADS, RW_HEAD), lambda i: (0, i, 0, 0)),
                  pl.BlockSpec((1, bb, RW_HEADS, RW_HEAD, RW_HEAD), lambda i: (layer, i, 0, 0, 0)),
                  _const_spec((RW_HEADS, RW_HEAD)), _const_spec((RW_HEADS, RW_HEAD))],
        out_specs=[pl.BlockSpec((bb, RW_HEADS, RW_HEAD, RW_HEAD), lambda i: (i, 0, 0, 0)),
                   pl.BlockSpec((bb, RW_HEADS, RW_HEAD), lambda i: (i, 0, 0))],
        out_shape=[jax.ShapeDtypeStruct((nb, RW_HEADS, RW_HEAD, RW_HEAD), F32),
                   jax.ShapeDtypeStruct((nb, RW_HEADS, RW_HEAD), F32)],
        compiler_params=_cparams(("parallel",)),
    )(vec, state, lnw.reshape(RW_HEADS, RW_HEAD), lnb.reshape(RW_HEADS, RW_HEAD))
    return y.reshape(nb, BRANCH_W), s_new


def _expand_heads():
    k = lax.broadcasted_iota(jnp.int32, (LANES, BRANCH_W), 0)
    c = lax.broadcasted_iota(jnp.int32, (LANES, BRANCH_W), 1)
    return jnp.where(jnp.right_shift(c, 6) == k, 1.0, 0.0).astype(BF16)


def _ssd_p_kernel(p_ref, cw_ref, cb_ref, dtb_ref, alog_ref, dsk_ref, nw_ref, y_ref, st_ref, buf_scr):
    cidx = pl.program_id(1)
    n = p_ref.shape[0]

    @pl.when(cidx == 0)
    def _():
        st_ref[...] = jnp.zeros_like(st_ref)
        buf_scr[0:8, :] = jnp.zeros((8, SSD_CONV_DIM), F32)

    z = p_ref[:, 0:BRANCH_W]
    buf_scr[8:8 + n, :] = p_ref[:, BRANCH_W:BRANCH_W + SSD_CONV_DIM]
    conv = cb_ref[...]
    for i in range(SSD_CONV):
        conv = conv + cw_ref[i:i + 1, :] * buf_scr[pl.ds(8 - (SSD_CONV - 1) + i, n), :]
    buf_scr[0:8, :] = buf_scr[n:n + 8, :]
    xa = _silu(conv)
    xs = xa[:, 0:BRANCH_W]
    bm = xa[:, BRANCH_W:BRANCH_W + SSD_GROUPS * SSD_STATE]
    cm = xa[:, BRANCH_W + SSD_GROUPS * SSD_STATE:]
    dt = _softplus(p_ref[:, BRANCH_W + SSD_CONV_DIM:SSD_PAD] + dtb_ref[...])
    dte = _dot_sel_r(dt, _expand_heads())
    da = dte * (-jnp.exp(alog_ref[...]))
    cum = _dot_sel_l(_tri_incl(n), da)
    xdt = xs * dte
    ecum = jnp.exp(cum)
    cl = cum[n - 1:n, :]
    xdl = xdt * jnp.exp(cl - cum)
    pl_ = jnp.exp(cl)

    m0, row, col = _pair_masks(n)
    incl = col <= row
    eye = col == row
    gw = BRANCH_W // SSD_GROUPS
    ppg = gw // LANES
    ys = []
    for gi in range(SSD_GROUPS):
        gs = slice(gi * gw, (gi + 1) * gw)
        bg = bm[:, gi * SSD_STATE:(gi + 1) * SSD_STATE]
        cg = cm[:, gi * SSD_STATE:(gi + 1) * SSD_STATE]
        cbp = _dgb(cg, jnp.concatenate([bg, bg], axis=0), NT)
        st = st_ref[0, :, gs]
        cs = _dgb(cg, st)
        for q in range(ppg):
            sl = slice(gi * gw + q * LANES, gi * gw + (q + 1) * LANES)
            cp = cum[:, sl]
            rp = jnp.sum(jnp.where(eye, cp, 0.0), axis=0, keepdims=True)
            seg = jnp.exp(jnp.where(incl, cp - rp, -jnp.inf))
            yp = _dgb(cbp * seg, _bd(xdt[:, sl], m0)) + ecum[:, sl] * cs[:, q * LANES:(q + 1) * LANES]
            ys.append(yp)
        st_ref[0, :, gs] = st * pl_[:, gs] + _dgb(bg, xdl[:, gs], TN)
    y = jnp.concatenate(ys, axis=1) + xs * dsk_ref[...]
    y = y * _silu(z)
    for gi in range(SSD_GROUPS):
        gs = slice(gi * gw, (gi + 1) * gw)
        yg = y[:, gs]
        yg = yg * lax.rsqrt(jnp.mean(yg * yg, axis=-1, keepdims=True) + GROUP_EPS)
        y_ref[:, gs] = (yg * nw_ref[:, gs]).astype(y_ref.dtype)


def _ssd_consts(conv_w, conv_b, dt_bias, a_log, d_skip, norm_w):
    dtb = jnp.pad(dt_bias, (0, LANES - SSD_HEADS)).reshape(1, LANES)
    return (conv_w, _row(conv_b), dtb, _row(jnp.repeat(a_log, SSD_HEAD)), _row(jnp.repeat(d_skip, SSD_HEAD)),
            _row(norm_w))


def _ssd_prompt(proj, nb, t, consts):
    nc = t // CHUNK
    return pl.pallas_call(
        _ssd_p_kernel,
        grid=(nb, nc),
        in_specs=[pl.BlockSpec((CHUNK, SSD_PAD), lambda b, c: (b * nc + c, 0))]
        + [_const_spec(x.shape) for x in consts],
        out_specs=[pl.BlockSpec((CHUNK, BRANCH_W), lambda b, c: (b * nc + c, 0)),
                   pl.BlockSpec((1, SSD_STATE, BRANCH_W), lambda b, c: (b, 0, 0))],
        out_shape=[jax.ShapeDtypeStruct((nb * t, BRANCH_W), BF16),
                   jax.ShapeDtypeStruct((nb, SSD_STATE, BRANCH_W), F32)],
        scratch_shapes=[pltpu.VMEM((CHUNK + 8, SSD_CONV_DIM), F32)],
        compiler_params=_cparams(("parallel", "arbitrary")),
    )(proj, *consts)


def _unpack_ssd_state(st):
    nb = st.shape[0]
    return jnp.transpose(st.reshape(nb, SSD_STATE, SSD_HEADS, SSD_HEAD), (0, 2, 3, 1))


def _ssd_s_prep_kernel(p_ref, cv_ref, cw_ref, cb_ref, dtb_ref, alog_ref, dsk_ref, o_ref, bc_ref):
    z = p_ref[:, 0:BRANCH_W]
    conv = cb_ref[...] + cw_ref[SSD_CONV - 1:SSD_CONV, :] * p_ref[:, BRANCH_W:BRANCH_W + SSD_CONV_DIM]
    for i in range(SSD_CONV - 1):
        conv = conv + cw_ref[i:i + 1, :] * cv_ref[i]
    xa = _silu(conv)
    xs = xa[:, 0:BRANCH_W]
    dt = _softplus(p_ref[:, BRANCH_W + SSD_CONV_DIM:SSD_PAD] + dtb_ref[...])
    dte = _dot_sel_r(dt, _expand_heads())
    o_ref[0] = xs * dte
    o_ref[1] = jnp.exp(dte * (-jnp.exp(alog_ref[...])))
    o_ref[2] = xs * dsk_ref[...]
    o_ref[3] = _silu(z)
    bc_ref[...] = xa[:, BRANCH_W:]


def _ssd_s_kernel(vec_ref, bc_ref, s_ref, nw_ref, so_ref, y_ref):
    nbb = s_ref.shape[1]
    n = SSD_HEAD
    eye = lax.broadcasted_iota(jnp.int32, (n, n), 0) == lax.broadcasted_iota(jnp.int32, (n, n), 1)
    hpg = SSD_HEADS // SSD_GROUPS

    def body(bi, carry):
        for h in range(SSD_HEADS):
            hs = pl.ds(h, 1)
            gi = h // hpg
            s = s_ref[0, bi, h]
            xdt_ = vec_ref[0, bi, hs, :]
            dec = vec_ref[1, bi, hs, 0:1]
            b_ = bc_ref[bi, pl.ds(gi, 1), :]
            c_ = bc_ref[bi, pl.ds(SSD_GROUPS + gi, 1), :]
            xc = jnp.sum(jnp.where(eye, xdt_, 0.0), axis=-1, keepdims=True)
            sn = s * dec + xc * b_
            so_ref[bi, h] = sn
            yc = jnp.sum(sn * c_, axis=-1, keepdims=True)
            yr = jnp.sum(jnp.where(eye, yc, 0.0), axis=0, keepdims=True)
            y_ref[bi, hs, :] = (yr + vec_ref[2, bi, hs, :]) * vec_ref[3, bi, hs, :]
        for gi in range(SSD_GROUPS):
            rs = pl.ds(gi * hpg, hpg)
            yg = y_ref[bi, rs, :]
            ms = jnp.sum(jnp.sum(yg * yg, axis=-1, keepdims=True), axis=0, keepdims=True) * (1.0 / (hpg * n))
            y_ref[bi, rs, :] = yg * lax.rsqrt(ms + GROUP_EPS) * nw_ref[rs, :]
        return carry

    lax.fori_loop(0, nbb, body, 0)


def _ssd_sample(proj, row0_blk, conv_prev, state, layer, consts):
    nb = conv_prev.shape[1]
    cw, cb, dtb, alog, dsk, nw = consts
    vec, bc = pl.pallas_call(
        _ssd_s_prep_kernel,
        grid=(1,),
        in_specs=[pl.BlockSpec((nb, SSD_PAD), lambda i: (row0_blk, 0)),
                  _const_spec(conv_prev.shape)] + [_const_spec(x.shape) for x in (cw, cb, dtb, alog, dsk)],
        out_specs=[_const_spec((4, nb, BRANCH_W)), _const_spec((nb, 2 * SSD_GROUPS * SSD_STATE))],
        out_shape=[jax.ShapeDtypeStruct((4, nb, BRANCH_W), F32),
                   jax.ShapeDtypeStruct((nb, 2 * SSD_GROUPS * SSD_STATE), F32)],
        compiler_params=_cparams(("arbitrary",)),
    )(proj, conv_prev, cw, cb, dtb, alog, dsk)
    vec = vec.reshape(4, nb, SSD_HEADS, SSD_HEAD)
    bc = bc.reshape(nb, 2 * SSD_GROUPS, SSD_STATE)
    bb = 8
    s_new, y = pl.pallas_call(
        _ssd_s_kernel,
        grid=(nb // bb,),
        in_specs=[pl.BlockSpec((4, bb, SSD_HEADS, SSD_HEAD), lambda i: (0, i, 0, 0)),
                  pl.BlockSpec((bb, 2 * SSD_GROUPS, SSD_STATE), lambda i: (i, 0, 0)),
                  pl.BlockSpec((1, bb, SSD_HEADS, SSD_HEAD, SSD_STATE), lambda i: (layer, i, 0, 0, 0)),
                  _const_spec((SSD_HEADS, SSD_HEAD))],
        out_specs=[pl.BlockSpec((bb, SSD_HEADS, SSD_HEAD, SSD_STATE), lambda i: (i, 0, 0, 0)),
                   pl.BlockSpec((bb, SSD_HEADS, SSD_HEAD), lambda i: (i, 0, 0))],
        out_shape=[jax.ShapeDtypeStruct((nb, SSD_HEADS, SSD_HEAD, SSD_STATE), F32),
                   jax.ShapeDtypeStruct((nb, SSD_HEADS, SSD_HEAD), F32)],
        compiler_params=_cparams(("parallel",)),
    )(vec, bc, state, nw.reshape(SSD_HEADS, SSD_HEAD))
    return y.reshape(nb, BRANCH_W), s_new


def _gla_tokens(p, au, ab):
    q = p[:, 0:GLA_DK] * (GLA_HK ** -0.5)
    k = p[:, GLA_DK:2 * GLA_DK]
    v = p[:, 2 * GLA_DK:2 * GLA_DK + GLA_DV]
    r = p[:, 2 * GLA_DK + GLA_DV:2 * GLA_DK + 2 * GLA_DV]
    ad = p[:, 2 * GLA_DK + 2 * GLA_DV:GLA_PAD]
    lg = -_softplus(-(_dgb(ad, au) + ab)) * (1.0 / GLA_TAU)
    return q, k, v, r, lg


def _gla_p_kernel(p_ref, au_ref, ab_ref, nw_ref, o_ref, st_ref):
    cidx = pl.program_id(1)

    @pl.when(cidx == 0)
    def _():
        st_ref[...] = jnp.zeros_like(st_ref)

    n = p_ref.shape[0]
    q, k, v, r, lg = _gla_tokens(p_ref[...], au_ref[...], ab_ref[...])
    cum = _dot_sel_l(_tri_incl(n), lg)
    cumx = cum - lg
    cl = cum[n - 1:n, :]
    qe = q * jnp.exp(cum)
    kb = k * jnp.exp(cl - cum)
    pl_ = jnp.exp(cl)
    rowi = lax.broadcasted_iota(jnp.int32, (SUB, 1), 0)
    for h in range(GLA_HEADS):
        sk = slice(h * GLA_HK, (h + 1) * GLA_HK)
        sv = slice(h * GLA_HV, (h + 1) * GLA_HV)
        st = st_ref[0, h]
        vh = v[:, sv]
        parts = []
        for blk in range(n // SUB):
            lo = blk * SUB
            rs = slice(lo, lo + SUB)
            q_i = q[rs, sk]
            c_i = cum[rs, sk]
            k_i = k[rs, sk]
            v_i = vh[rs]
            if blk > 0:
                cref = cumx[lo:lo + 1, sk]
                qt = q_i * jnp.exp(c_i - cref)
                kt = k[0:lo, sk] * jnp.exp(cref - cum[0:lo, sk])
                acc = _dgb(_dgb(qt, kt, NT), vh[0:lo])
            else:
                acc = jnp.zeros((SUB, GLA_HV), F32)
            for s in range(SUB):
                e = jnp.exp(c_i - c_i[s:s + 1])
                w = jnp.sum(q_i * k_i[s:s + 1] * e, axis=-1, keepdims=True)
                acc = acc + jnp.where(rowi >= s, w, 0.0) * v_i[s:s + 1]
            parts.append(acc)
        o = jnp.concatenate(parts, axis=0) + _dgb(qe[:, sk], st, NT)
        st_ref[0, h] = st * pl_[:, sk] + _dgb(vh, kb[:, sk], TN)
        o = o * lax.rsqrt(jnp.mean(o * o, axis=-1, keepdims=True) + GROUP_EPS) * nw_ref[...]
        o_ref[:, sv] = (o * _silu(r[:, sv])).astype(o_ref.dtype)


def _gla_consts(alpha_up, alpha_b, norm_w):
    au = jnp.pad(alpha_up, ((0, LANES - GLA_LORA), (0, 0))).astype(BF16)
    return au, _row(alpha_b), _row(norm_w)


def _gla_prompt(proj, nb, t, consts):
    nc = t // CHUNK
    return pl.pallas_call(
        _gla_p_kernel,
        grid=(nb, nc),
        in_specs=[pl.BlockSpec((CHUNK, GLA_PAD), lambda b, c: (b * nc + c, 0))]
        + [_const_spec(x.shape) for x in consts],
        out_specs=[pl.BlockSpec((CHUNK, GLA_DV), lambda b, c: (b * nc + c, 0)),
                   pl.BlockSpec((1, GLA_HEADS, GLA_HV, GLA_HK), lambda b, c: (b, 0, 0, 0))],
        out_shape=[jax.ShapeDtypeStruct((nb * t, GLA_DV), BF16),
                   jax.ShapeDtypeStruct((nb, GLA_HEADS, GLA_HV, GLA_HK), F32)],
        compiler_params=_cparams(("parallel", "arbitrary")),
    )(proj, *consts)


def _gla_s_prep_kernel(p_ref, au_ref, ab_ref, qk_ref, vr_ref):
    q, k, v, r, lg = _gla_tokens(p_ref[...], au_ref[...], ab_ref[...])
    qk_ref[0] = q
    qk_ref[1] = k
    qk_ref[2] = jnp.exp(lg)
    vr_ref[0] = v
    vr_ref[1] = _silu(r)


def _gla_s_kernel(qk_ref, vr_ref, s_ref, nw_ref, so_ref, y_ref):
    nbb = s_ref.shape[1]
    n = GLA_HK
    eye = lax.broadcasted_iota(jnp.int32, (n, n), 0) == lax.broadcasted_iota(jnp.int32, (n, n), 1)

    def col(x):
        return jnp.sum(jnp.where(eye, x, 0.0), axis=-1, keepdims=True)

    def body(bi, carry):
        for h in range(GLA_HEADS):
            hs = pl.ds(h, 1)
            s = s_ref[0, bi, h]
            sn = s * col(qk_ref[2, bi, hs, :]) + col(qk_ref[1, bi, hs, :]) * vr_ref[0, bi, hs, :]
            so_ref[bi, h] = sn
            o = jnp.sum(sn * col(qk_ref[0, bi, hs, :]), axis=0, keepdims=True)
            o = o * lax.rsqrt(jnp.mean(o * o, axis=-1, keepdims=True) + GROUP_EPS) * nw_ref[...]
            y_ref[bi, hs, :] = o * vr_ref[1, bi, hs, :]
        return carry

    lax.fori_loop(0, nbb, body, 0)


def _gla_sample(proj, row0_blk, nb, state, layer, consts):
    au, ab, nw = consts
    qk, vr = pl.pallas_call(
        _gla_s_prep_kernel,
        grid=(1,),
        in_specs=[pl.BlockSpec((nb, GLA_PAD), lambda i: (row0_blk, 0)), _const_spec(au.shape),
                  _const_spec(ab.shape)],
        out_specs=[_const_spec((3, nb, GLA_DK)), _const_spec((2, nb, GLA_DV))],
        out_shape=[jax.ShapeDtypeStruct((3, nb, GLA_DK), F32), jax.ShapeDtypeStruct((2, nb, GLA_DV), F32)],
        compiler_params=_cparams(("arbitrary",)),
    )(proj, au, ab)
    qk = qk.reshape(3, nb, GLA_HEADS, GLA_HK)
    vr = vr.reshape(2, nb, GLA_HEADS, GLA_HV)
    bb = 8
    s_new, y = pl.pallas_call(
        _gla_s_kernel,
        grid=(nb // bb,),
        in_specs=[pl.BlockSpec((3, bb, GLA_HEADS, GLA_HK), lambda i: (0, i, 0, 0)),
                  pl.BlockSpec((2, bb, GLA_HEADS, GLA_HV), lambda i: (0, i, 0, 0)),
                  pl.BlockSpec((1, bb, GLA_HEADS, GLA_HK, GLA_HV), lambda i: (layer, i, 0, 0, 0)),
                  _const_spec(nw.shape)],
        out_specs=[pl.BlockSpec((bb, GLA_HEADS, GLA_HK, GLA_HV), lambda i: (i, 0, 0, 0)),
                   pl.BlockSpec((bb, GLA_HEADS, GLA_HV), lambda i: (i, 0, 0))],
        out_shape=[jax.ShapeDtypeStruct((nb, GLA_HEADS, GLA_HK, GLA_HV), F32),
                   jax.ShapeDtypeStruct((nb, GLA_HEADS, GLA_HV), F32)],
        compiler_params=_cparams(("parallel",)),
    )(qk, vr, state, nw)
    return y.reshape(nb, GLA_DV), s_new


def _pad_cols(w, n):
    return jnp.pad(w, ((0, 0), (0, n - w.shape[1])))


def kernel(x_prompt, x_sample, state_rwkv, state_rwkv_shift, state_ssd, state_ssd_conv, state_gla, cache_mem_k, cache_mem_v, mem_prompt, norm_mix_pre, norm_mix_post, norm_x_pre, norm_x_post, norm_ffn_pre, norm_ffn_post, w_in, rw_mu, rw_w0, rw_w2, rw_a0, rw_a2, rw_g2, rw_kk, rw_ka, rw_rk, rw_ln_w, rw_ln_b, ssd_conv_w, ssd_conv_b, ssd_dt_bias, ssd_a_log, ssd_d, ssd_norm_w, gla_alpha_up, gla_alpha_b, gla_norm_w, w_branch, w_out, x_mem_norm, x_wq, x_wk, x_wv, x_wo, ffn_up, ffn_down):
    nbp, t, d = x_prompt.shape
    nbs = x_sample.shape[0]
    depth = w_in.shape[0]
    mp = nbp * t
    m = mp + nbs
    sblk = mp // nbs
    o1 = RW_PROJ
    o2 = o1 + SSD_PROJ
    o3 = o2 + GLA_PROJ

    x = jnp.concatenate([x_prompt.reshape(mp, d), x_sample.reshape(nbs, d)], axis=0)
    mem_rows = mem_prompt.reshape(nbp * N_MEM, d)
    ck = cache_mem_k.reshape(depth, nbs, N_MEM, d)
    cv = cache_mem_v.reshape(depth, nbs, N_MEM, d)
    tm_norm = _row_tile(m, (1040, 1024, 512, 256, 128, 64, 8))

    xn = _norm_rows(x, norm_mix_pre[0], tm_norm)
    p_acc = [[] for _ in range(7)]
    s_acc = [[] for _ in range(5)]
    for l in range(depth):
        wl = w_in[l]
        w_rw = _pad_cols(wl[:, :o1], RW_PAD).astype(BF16)
        w_ssd = _pad_cols(wl[:, o1:o2], SSD_PAD).astype(BF16)
        w_gla = _pad_cols(wl[:, o2:o3], GLA_PAD).astype(BF16)
        w_gate = wl[:, o3:].astype(BF16)

        proj_rw = _mm(xn, w_rw)
        proj_ssd = _mm(xn, w_ssd)
        proj_gla = _mm(xn, w_gla)
        gate = _mm(xn, w_gate)

        w2p, a2p, g2p = _rwkv_weights(rw_w2[l], rw_a2[l], rw_g2[l])
        rw_tok = (_row(jnp.pad(rw_mu[l], (0, RW_PAD - RW_PROJ))), _row(rw_w0[l]), w2p, _row(rw_a0[l]), a2p, g2p,
                  _row(rw_kk[l]), _row(rw_ka[l]), _row(rw_rk[l]))
        ya_p, rw_sp = _rwkv_prompt(proj_rw, nbp, t, rw_tok + (_row(rw_ln_w[l]), _row(rw_ln_b[l])))
        shift_prev = jnp.pad(state_rwkv_shift[l], ((0, 0), (0, RW_PAD - RW_PROJ)))
        ya_s, rw_ss = _rwkv_sample(proj_rw, sblk, shift_prev, state_rwkv, l, rw_tok, rw_ln_w[l], rw_ln_b[l])

        ssd_c = _ssd_consts(ssd_conv_w[l], ssd_conv_b[l], ssd_dt_bias[l], ssd_a_log[l], ssd_d[l], ssd_norm_w[l])
        yb_p, ssd_sp = _ssd_prompt(proj_ssd, nbp, t, ssd_c)
        conv_prev = jnp.transpose(state_ssd_conv[l], (1, 0, 2))
        yb_s, ssd_ss = _ssd_sample(proj_ssd, sblk, conv_prev, state_ssd, l, ssd_c)

        gla_c = _gla_consts(gla_alpha_up[l], gla_alpha_b[l], gla_norm_w[l])
        yc_p, gla_sp = _gla_prompt(proj_gla, nbp, t, gla_c)
        yc_s, gla_ss = _gla_sample(proj_gla, sblk, nbs, state_gla, l, gla_c)

        y3 = jnp.stack([jnp.concatenate([ya_p, ya_s.astype(BF16)], axis=0),
                        jnp.concatenate([yb_p, yb_s.astype(BF16)], axis=0),
                        jnp.concatenate([yc_p, yc_s.astype(BF16)], axis=0)])
        mix = _merge(y3, w_branch[l].astype(BF16), gate)
        x, xn = _mm_resnorm(mix, w_out[l].astype(BF16), x, norm_mix_post[l], norm_x_pre[l])

        mn = _norm_rows(mem_rows, x_mem_norm[l], _row_tile(mem_rows.shape[0], (1024, 512, 256, 128, 64, 8)))
        mk = _mm(mn, x_wk[l].astype(BF16))
        mv = _mm(mn, x_wv[l].astype(BF16))
        q = _mm(xn, x_wq[l].astype(BF16), out_dtype=BF16)
        o_p = _attn_prompt(q, mk.reshape(nbp, N_MEM, d), mv.reshape(nbp, N_MEM, d), nbp, t)
        o_s = _attn_sample(q[mp:].reshape(nbs, 1, d), ck, cv, l)
        o = jnp.concatenate([o_p, o_s.reshape(nbs, d)], axis=0)
        x, xn = _mm_resnorm(o, x_wo[l].astype(BF16), x, norm_x_post[l], norm_ffn_pre[l])

        hf = _mm(xn, ffn_up[l].astype(BF16), out_dtype=BF16, act="relu2")
        g_next = norm_mix_pre[l + 1] if l + 1 < depth else norm_mix_pre[l]
        x, xn = _mm_resnorm(hf, ffn_down[l].astype(BF16), x, norm_ffn_post[l], g_next)

        pr = proj_rw[:mp].reshape(nbp, t, RW_PAD)
        ps_ = proj_ssd[:mp].reshape(nbp, t, SSD_PAD)
        new_p = (_unpack_rwkv_state(rw_sp), pr[:, t - 1, :RW_PROJ], _unpack_ssd_state(ssd_sp),
                 ps_[:, t - (SSD_CONV - 1):, BRANCH_W:BRANCH_W + SSD_CONV_DIM],
                 jnp.transpose(gla_sp, (0, 1, 3, 2)),
                 mk.reshape(nbp, N_MEM, X_HEADS, X_HEAD), mv.reshape(nbp, N_MEM, X_HEADS, X_HEAD))
        xbc_s = proj_ssd[mp:, BRANCH_W:BRANCH_W + SSD_CONV_DIM]
        new_s = (rw_ss, proj_rw[mp:, :RW_PROJ], ssd_ss,
                 jnp.concatenate([state_ssd_conv[l][:, 1:], xbc_s[:, None, :]], axis=1), gla_ss)
        for acc, val in zip(p_acc, new_p):
            acc.append(val)
        for acc, val in zip(s_acc, new_s):
            acc.append(val)

    outs_p = [jnp.stack(a) for a in p_acc]
    outs_s = [jnp.stack(a) for a in s_acc]
    return (x[:mp].reshape(nbp, t, d), x[mp:].reshape(nbs, 1, d), *outs_p, *outs_s)
```

```python
import functools

import jax
import jax.numpy as jnp
from jax import lax
from jax.experimental import pallas as pl
from jax.experimental.pallas import tpu as pltpu

F32 = jnp.float32
BF16 = jnp.bfloat16

D_MODEL = 2048
BRANCH_W = 1024
N_BRANCH = 3
RW_HEADS = 16
RW_HEAD = 64
RW_PROJ = 3360
RW_PAD = 3456
RW_GN_EPS = 64e-5
SSD_HEADS = 16
SSD_HEAD = 64
SSD_GROUPS = 2
SSD_STATE = 128
SSD_CONV = 4
SSD_CONV_DIM = 1536
SSD_PROJ = 2576
SSD_PAD = 2688
GLA_HEADS = 4
GLA_DK = 512
GLA_DV = 1024
GLA_HK = 128
GLA_HV = 256
GLA_LORA = 16
GLA_TAU = 16.0
GLA_PROJ = 3088
GLA_PAD = 3200
N_MEM = 256
X_HEADS = 4
X_HEAD = 512
D_FF = 8192
NORM_EPS = 1e-6
GROUP_EPS = 1e-5
CHUNK = 64
SUB = 16
LANES = 128
VMEM_LIMIT = 56 * 1024 * 1024

NN = ((1,), (0,))
NT = ((1,), (1,))
TN = ((0,), (0,))


def _cparams(sem):
    return pltpu.CompilerParams(dimension_semantics=sem, vmem_limit_bytes=VMEM_LIMIT)


def _dg(a, b, dims=NN):
    return lax.dot_general(a, b, (dims, ((), ())), preferred_element_type=F32)


def _dgb(a, b, dims=NN):
    return _dg(a.astype(BF16), b.astype(BF16), dims)


def _hl(x):
    h = x.astype(BF16)
    return h, (x - h.astype(F32)).astype(BF16)


def _dot_hi(a, b, dims=NN):
    ah, al = _hl(a)
    bh, bl = _hl(b)
    return _dg(ah, bh, dims) + (_dg(ah, bl, dims) + _dg(al, bh, dims))


def _split3(x):
    h = x.astype(BF16)
    r = x - h.astype(F32)
    m = r.astype(BF16)
    return h, m, (r - m.astype(F32)).astype(BF16)


def _dot_sel_l(sel, x):
    h, m, l = _split3(x)
    return _dg(sel, h) + (_dg(sel, m) + _dg(sel, l))


def _dot_sel_r(x, sel):
    h, m, l = _split3(x)
    return _dg(h, sel) + (_dg(m, sel) + _dg(l, sel))


def _softplus(x):
    return jnp.maximum(x, 0.0) + jnp.log1p(jnp.exp(-jnp.abs(x)))


def _sigmoid(x):
    return 1.0 / (1.0 + jnp.exp(-x))


def _silu(x):
    return x * _sigmoid(x)


def _tri_incl(n):
    r = lax.broadcasted_iota(jnp.int32, (n, n), 0)
    c = lax.broadcasted_iota(jnp.int32, (n, n), 1)
    return jnp.where(c <= r, 1.0, 0.0).astype(BF16)


def _pair_masks(rows):
    lane = lax.broadcasted_iota(jnp.int32, (rows, LANES), 1)
    row = lax.broadcasted_iota(jnp.int32, (rows, LANES), 0)
    m0 = lane < 64
    col = jnp.bitwise_and(lane, 63)
    return m0, row, col


def _headsum_pair(x, m0):
    s0 = jnp.sum(jnp.where(m0, x, 0.0), axis=-1, keepdims=True)
    s1 = jnp.sum(jnp.where(m0, 0.0, x), axis=-1, keepdims=True)
    return jnp.where(m0, s0, s1)


def _bd(x, m0):
    return jnp.concatenate([jnp.where(m0, x, 0.0), jnp.where(m0, 0.0, x)], axis=0)


def _norm_kernel(x_ref, g_ref, o_ref):
    x = x_ref[...]
    y = x * lax.rsqrt(jnp.mean(x * x, axis=-1, keepdims=True) + NORM_EPS)
    o_ref[...] = (y * g_ref[...]).astype(o_ref.dtype)


def _norm_rows(x, g, tm):
    m, d = x.shape
    return pl.pallas_call(
        _norm_kernel,
        grid=(m // tm,),
        in_specs=[pl.BlockSpec((tm, d), lambda i: (i, 0)), pl.BlockSpec((1, d), lambda i: (0, 0))],
        out_specs=pl.BlockSpec((tm, d), lambda i: (i, 0)),
        out_shape=jax.ShapeDtypeStruct((m, d), BF16),
        compiler_params=_cparams(("parallel",)),
        name="rmsnorm",
    )(x, g.reshape(1, d))


def _mm_kernel(a_ref, w_ref, o_ref, *scratch, nk, act):
    part = _dg(a_ref[...], w_ref[0])

    def finish(acc):
        if act == "relu2":
            acc = jnp.square(jnp.maximum(acc, 0.0))
        o_ref[...] = acc.astype(o_ref.dtype)

    if nk == 1:
        finish(part)
        return
    acc_ref, = scratch
    k = pl.program_id(2)

    @pl.when(k == 0)
    def _():
        acc_ref[...] = part

    @pl.when(k > 0)
    def _():
        acc_ref[...] += part

    @pl.when(k == nk - 1)
    def _():
        finish(acc_ref[...])


def _mm_resnorm_kernel(a_ref, w_ref, res_ref, gp_ref, gn_ref, x_ref, xn_ref, *scratch, nk):
    part = _dg(a_ref[...], w_ref[0])

    def finish(acc):
        y = acc * lax.rsqrt(jnp.mean(acc * acc, axis=-1, keepdims=True) + NORM_EPS) * gp_ref[...]
        x = res_ref[...] + y
        x_ref[...] = x
        xn = x * lax.rsqrt(jnp.mean(x * x, axis=-1, keepdims=True) + NORM_EPS) * gn_ref[...]
        xn_ref[...] = xn.astype(xn_ref.dtype)

    if nk == 1:
        finish(part)
        return
    acc_ref, = scratch
    k = pl.program_id(1)

    @pl.when(k == 0)
    def _():
        acc_ref[...] = part

    @pl.when(k > 0)
    def _():
        acc_ref[...] += part

    @pl.when(k == nk - 1)
    def _():
        finish(acc_ref[...])


def _pick(n, cands):
    for c in cands:
        if n % c == 0:
            return c
    return n


def _row_tile(m, cands):
    return _pick(m, cands)


def _mm(a, w, layer, out_dtype=F32, act=None, name="mm"):
    m, kd = a.shape
    n = w.shape[2]
    tm = _row_tile(m, (1040, 1024, 512, 256, 128, 64, 8))
    tn = _pick(n, (1024, 1152, 896, 640, 512, 384, 256, 128))
    tk = _pick(kd, (2048, 1024, 512))
    nk = kd // tk
    scratch = [pltpu.VMEM((tm, tn), F32)] if nk > 1 else []
    return pl.pallas_call(
        functools.partial(_mm_kernel, nk=nk, act=act),
        grid=(m // tm, n // tn, nk),
        in_specs=[pl.BlockSpec((tm, tk), lambda i, j, k: (i, k)),
                  pl.BlockSpec((1, tk, tn), lambda i, j, k: (layer, k, j))],
        out_specs=pl.BlockSpec((tm, tn), lambda i, j, k: (i, j)),
        out_shape=jax.ShapeDtypeStruct((m, n), out_dtype),
        scratch_shapes=scratch,
        compiler_params=_cparams(("parallel", "parallel", "arbitrary")),
        name=name,
    )(a, w)


def _mm_resnorm(a, w, layer, res, g_post, g_next, name="mm_resnorm"):
    m, kd = a.shape
    n = w.shape[2]
    tm = _row_tile(m, (520, 512, 256, 128, 64, 8))
    tk = _pick(kd, (1024, 512))
    nk = kd // tk
    scratch = [pltpu.VMEM((tm, n), F32)] if nk > 1 else []
    return pl.pallas_call(
        functools.partial(_mm_resnorm_kernel, nk=nk),
        grid=(m // tm, nk),
        in_specs=[pl.BlockSpec((tm, tk), lambda i, k: (i, k)),
                  pl.BlockSpec((1, tk, n), lambda i, k: (layer, k, 0)),
                  pl.BlockSpec((tm, n), lambda i, k: (i, 0)),
                  pl.BlockSpec((1, n), lambda i, k: (0, 0)),
                  pl.BlockSpec((1, n), lambda i, k: (0, 0))],
        out_specs=[pl.BlockSpec((tm, n), lambda i, k: (i, 0)),
                   pl.BlockSpec((tm, n), lambda i, k: (i, 0))],
        out_shape=[jax.ShapeDtypeStruct((m, n), F32), jax.ShapeDtypeStruct((m, n), BF16)],
        scratch_shapes=scratch,
        compiler_params=_cparams(("parallel", "arbitrary")),
        name=name,
    )(a, w, res, g_post.reshape(1, n), g_next.reshape(1, n))


def _merge_kernel(y_ref, w_ref, g_ref, o_ref, acc_ref):
    n = pl.program_id(2)
    z = _dg(y_ref[0], w_ref[0, 0]) * _sigmoid(g_ref[...])

    @pl.when(n == 0)
    def _():
        acc_ref[...] = z

    @pl.when(n > 0)
    def _():
        acc_ref[...] += z

    @pl.when(n == N_BRANCH - 1)
    def _():
        o_ref[...] = acc_ref[...].astype(o_ref.dtype)


def _merge(y3, wb, layer, gate_logits):
    _, m, bw = y3.shape
    d = wb.shape[3]
    tm = _row_tile(m, (1040, 1024, 512, 256, 128, 64, 8))
    tn = 1024
    nj = d // tn
    return pl.pallas_call(
        _merge_kernel,
        grid=(m // tm, nj, N_BRANCH),
        in_specs=[pl.BlockSpec((1, tm, bw), lambda i, j, n: (n, i, 0)),
                  pl.BlockSpec((1, 1, bw, tn), lambda i, j, n: (layer, n, 0, j)),
                  pl.BlockSpec((tm, tn), lambda i, j, n: (i, n * nj + j))],
        out_specs=pl.BlockSpec((tm, tn), lambda i, j, n: (i, j)),
        out_shape=jax.ShapeDtypeStruct((m, d), BF16),
        scratch_shapes=[pltpu.VMEM((tm, tn), F32)],
        compiler_params=_cparams(("parallel", "parallel", "arbitrary")),
        name="merge",
    )(y3, wb, gate_logits)


def _cast_kernel(w_ref, o_ref):
    o_ref[...] = w_ref[...].astype(o_ref.dtype)


def _cast_bf16(w):
    g, r, c = w.shape
    tr = _pick(r, (256, 128, 64, 8))
    return pl.pallas_call(
        _cast_kernel,
        grid=(g, r // tr),
        in_specs=[pl.BlockSpec((1, tr, c), lambda a, i: (a, i, 0))],
        out_specs=pl.BlockSpec((1, tr, c), lambda a, i: (a, i, 0)),
        out_shape=jax.ShapeDtypeStruct((g, r, c), BF16),
        compiler_params=_cparams(("parallel", "parallel")),
        name="cast_bf16",
    )(w)


_SEGS = ((0, RW_PROJ, RW_PAD),
         (RW_PROJ, SSD_PROJ, SSD_PAD),
         (RW_PROJ + SSD_PROJ, GLA_PROJ, GLA_PAD),
         (RW_PROJ + SSD_PROJ + GLA_PROJ, N_BRANCH * D_MODEL, N_BRANCH * D_MODEL))


def _seg_window(first):
    base = first // LANES * LANES
    return base, first - base


def _split_kernel(*refs):
    ins, outs = refs[:len(_SEGS)], refs[len(_SEGS):]
    for (first, live, padded), w_ref, o_ref in zip(_SEGS, ins, outs):
        _, off = _seg_window(first)
        width = -(-(off + live) // LANES) * LANES
        x = w_ref[0, :, 0:width]
        if off:
            x = pltpu.roll(x, width - off, 1)
        x = x[:, 0:padded]
        if live < padded:
            lane = lax.broadcasted_iota(jnp.int32, x.shape, 1)
            x = jnp.where(lane < live, x, 0.0)
        o_ref[0] = x.astype(o_ref.dtype)


def _split_w_in(w_in):
    depth, d, n_in = w_in.shape
    tr = 128
    in_specs = []
    for first, live, padded in _SEGS:
        base, off = _seg_window(first)
        bw = base if base else padded
        assert off + live <= bw and (base == 0 or base % bw == 0)
        in_specs.append(pl.BlockSpec((1, tr, bw), lambda l, i, b=(base // bw): (l, i, b)))
    return pl.pallas_call(
        _split_kernel,
        grid=(depth, d // tr),
        in_specs=in_specs,
        out_specs=[pl.BlockSpec((1, tr, p), lambda l, i: (l, i, 0)) for _, _, p in _SEGS],
        out_shape=[jax.ShapeDtypeStruct((depth, d, p), BF16) for _, _, p in _SEGS],
        compiler_params=_cparams(("parallel", "parallel")),
        name="split_w_in",
    )(*([w_in] * len(_SEGS)))


def _attn_p_kernel(q_ref, k_ref, v_ref, o_ref):
    s = _dgb(q_ref[...], k_ref[0], NT) * (X_HEAD ** -0.5)
    p = jnp.exp(s - jnp.max(s, axis=-1, keepdims=True))
    attn = p / jnp.sum(p, axis=-1, keepdims=True)
    o_ref[...] = _dgb(attn, v_ref[0]).astype(o_ref.dtype)


def _attn_prompt(q, mem_k, mem_v, nb, t):
    tq = _pick(t, (1024, 512, 256, 128, 64))
    nq = t // tq
    return pl.pallas_call(
        _attn_p_kernel,
        grid=(nb, X_HEADS, nq),
        in_specs=[pl.BlockSpec((tq, X_HEAD), lambda b, h, i: (b * nq + i, h)),
                  pl.BlockSpec((1, N_MEM, X_HEAD), lambda b, h, i: (b, 0, h)),
                  pl.BlockSpec((1, N_MEM, X_HEAD), lambda b, h, i: (b, 0, h))],
        out_specs=pl.BlockSpec((tq, X_HEAD), lambda b, h, i: (b * nq + i, h)),
        out_shape=jax.ShapeDtypeStruct((nb * t, D_MODEL), BF16),
        compiler_params=_cparams(("parallel", "parallel", "parallel")),
        name="attn_prompt",
    )(q, mem_k, mem_v)


def _attn_s_kernel(q_ref, k_ref, v_ref, o_ref):
    q = q_ref[0].astype(F32)
    s = jnp.sum(k_ref[0, 0] * q[None], axis=-1, keepdims=True) * (X_HEAD ** -0.5)
    p = jnp.exp(s - jnp.max(s, axis=0, keepdims=True))
    attn = p / jnp.sum(p, axis=0, keepdims=True)
    o_ref[0] = jnp.sum(attn * v_ref[0, 0], axis=0).astype(o_ref.dtype)


def _attn_sample(q3, cache_k, cache_v, layer):
    nb = q3.shape[0]
    kv_spec = pl.BlockSpec((1, 1, N_MEM, X_HEADS, X_HEAD), lambda b: (layer, b, 0, 0, 0))
    return pl.pallas_call(
        _attn_s_kernel,
        grid=(nb,),
        in_specs=[pl.BlockSpec((1, X_HEADS, X_HEAD), lambda b: (b, 0, 0)), kv_spec, kv_spec],
        out_specs=pl.BlockSpec((1, X_HEADS, X_HEAD), lambda b: (b, 0, 0)),
        out_shape=jax.ShapeDtypeStruct((nb, X_HEADS, X_HEAD), BF16),
        compiler_params=_cparams(("parallel",)),
        name="attn_sample",
    )(q3, cache_k, cache_v)


def _rwkv_tokens(p, prev, mu, w0, w2, a0, a2, g2, k_k, k_a):
    c = BRANCH_W
    ps = p + (prev - p) * mu
    r = ps[:, 0:c]
    k = ps[:, c:2 * c]
    v = ps[:, 2 * c:3 * c]
    slab = ps[:, 3 * c:3 * c + 128]
    gslab = ps[:, 3 * c + 128:RW_PAD]
    wl = w0 + _dgb(jnp.tanh(slab), w2)
    lw = -jnp.exp(-_softplus(-wl) - 0.5)
    a = _sigmoid(a0 + _dgb(slab, a2))
    g = _dgb(_sigmoid(gslab), g2)
    kkraw = k * k_k
    k2 = k * (1.0 + (a - 1.0) * k_a)
    return r, k2, v, lw, a, g, kkraw


def _rwkv_p_kernel(p_ref, mu_ref, w0_ref, w2_ref, a0_ref, a2_ref, g2_ref, kk_ref, ka_ref, rk_ref,
                   lnw_ref, lnb_ref, y_ref, s_ref, prev_scr):
    cidx = pl.program_id(1)

    @pl.when(cidx == 0)
    def _():
        s_ref[...] = jnp.zeros_like(s_ref)
        prev_scr[...] = jnp.zeros_like(prev_scr)

    p = p_ref[...]
    n = p.shape[0]
    row1 = lax.broadcasted_iota(jnp.int32, (n, 1), 0)
    prev = jnp.where(row1 == 0, prev_scr[0:1, :], pltpu.roll(p, 1, 0))
    prev_scr[0:1, :] = p[n - 1:n, :]
    r, k2, v, lw, a, g, kkraw = _rwkv_tokens(p, prev, mu_ref[...], w0_ref[...], w2_ref[...], a0_ref[...],
                                             a2_ref[...], g2_ref[...], kk_ref[...], ka_ref[...])
    cum = _dot_sel_l(_tri_incl(n), lw)
    e_c = jnp.exp(cum)
    e_x = jnp.exp(cum - lw)
    e_n = jnp.exp(-cum)
    e_l = jnp.exp(cum[n - 1:n, :] - cum)
    rk = rk_ref[...]
    lnw = lnw_ref[...]
    lnb = lnb_ref[...]

    m0, row, col = _pair_masks(n)
    strict = col < row
    incl = col <= row
    eye = jnp.where(col == row, 1.0, 0.0)
    ms, _, _ = _pair_masks(RW_HEAD)

    prs = range(RW_HEADS // 2)
    sls = [slice(pi * LANES, (pi + 1) * LANES) for pi in prs]
    kkn = []
    for sl in sls:
        kkp = kkraw[:, sl]
        kkn.append(kkp / jnp.maximum(jnp.sqrt(_headsum_pair(kkp * kkp, m0)), 1e-12))
    bv = [kkn[i] * a[:, sls[i]] for i in prs]
    at = [-kkn[i] * e_x[:, sls[i]] for i in prs]
    rt = [r[:, sl] * e_c[:, sl] for sl in sls]
    vv = [v[:, sl] for sl in sls]
    ar = [jnp.concatenate([at[i], rt[i]], axis=0) for i in prs]
    sab = [_dot_hi(ar[i], _bd(bv[i] * e_n[:, sls[i]], m0), NT) for i in prs]
    sak = [_dot_hi(ar[i], _bd(k2[:, sls[i]] * e_n[:, sls[i]], m0), NT) for i in prs]
    a_ab = [jnp.where(strict, s[:n], 0.0) for s in sab]
    a_ak = [jnp.where(strict, s[:n], 0.0) for s in sak]
    m_rb = [jnp.where(incl, s[n:], 0.0) for s in sab]
    m_rk = [jnp.where(incl, s[n:], 0.0) for s in sak]
    tinv = [eye + x for x in a_ab]
    xs = a_ab
    akv = [_dot_hi(a_ak[i], _bd(vv[i], m0)) for i in prs]
    for _ in range(n.bit_length() - 2):
        xs = [_dot_hi(x, _bd(x, m0)) for x in xs]
        tinv = [tinv[i] + _dot_hi(tinv[i], _bd(xs[i], m0)) for i in prs]
    tw = [_dot_hi(tinv[i], jnp.concatenate([_bd(at[i], m0), _bd(akv[i], m0)], axis=1)) for i in prs]
    s0 = [s_ref[0, pi] for pi in prs]
    us = [_dot_hi(jnp.concatenate([tw[i][:, :LANES], rt[i]], axis=0), _bd(s0[i], ms), NT) for i in prs]
    u = [us[i][:n] + tw[i][:, LANES:] for i in prs]
    uv = [jnp.concatenate([u[i], vv[i]], axis=0) for i in prs]
    y = [us[i][n:] + _dot_hi(jnp.concatenate([m_rb[i], m_rk[i]], axis=1),
                             jnp.concatenate([_bd(u[i], m0), _bd(vv[i], m0)], axis=0)) for i in prs]
    for i in prs:
        sl = sls[i]
        bk = jnp.concatenate([bv[i] * e_l[:, sl], k2[:, sl] * e_l[:, sl]], axis=0)
        z = _dot_hi(uv[i], bk, TN)
        s_ref[0, i] = s0[i] * e_c[n - 1:n, sl] + jnp.where(ms, z[:RW_HEAD], z[RW_HEAD:])
    for i in prs:
        sl = sls[i]
        mean = _headsum_pair(y[i], m0) * (1.0 / RW_HEAD)
        d = y[i] - mean
        var = _headsum_pair(d * d, m0) * (1.0 / RW_HEAD)
        yn = d * lax.rsqrt(var + RW_GN_EPS) * lnw[:, sl] + lnb[:, sl]
        bonus = _headsum_pair(r[:, sl] * k2[:, sl] * rk[:, sl], m0) * vv[i]
        y_ref[:, sl] = ((yn + bonus) * g[:, sl]).astype(y_ref.dtype)


def _rwkv_weights(w2, a2, g2):
    lw = w2.shape[0]
    w2p = jnp.concatenate([w2, jnp.zeros((LANES - lw, BRANCH_W), F32)], axis=0).astype(BF16)
    a2p = jnp.concatenate([jnp.zeros((lw, BRANCH_W), F32), a2], axis=0).astype(BF16)
    g2p = jnp.pad(g2, ((0, RW_PAD - 3 * BRANCH_W - LANES - g2.shape[0]), (0, 0))).astype(BF16)
    return w2p, a2p, g2p


def _row(x):
    return x.reshape(1, -1)


def _const_spec(shape):
    nd = len(shape)
    return pl.BlockSpec(shape, lambda *_: (0,) * nd)


def _rwkv_prompt(proj, nb, t, wts):
    nc = t // CHUNK
    consts = wts
    in_specs = [pl.BlockSpec((CHUNK, RW_PAD), lambda b, c: (b * nc + c, 0))]
    in_specs += [_const_spec(x.shape) for x in consts]
    return pl.pallas_call(
        _rwkv_p_kernel,
        grid=(nb, nc),
        in_specs=in_specs,
        out_specs=[pl.BlockSpec((CHUNK, BRANCH_W), lambda b, c: (b * nc + c, 0)),
                   pl.BlockSpec((1, RW_HEADS // 2, RW_HEAD, LANES), lambda b, c: (b, 0, 0, 0))],
        out_shape=[jax.ShapeDtypeStruct((nb * t, BRANCH_W), BF16),
                   jax.ShapeDtypeStruct((nb, RW_HEADS // 2, RW_HEAD, LANES), F32)],
        scratch_shapes=[pltpu.VMEM((8, RW_PAD), F32)],
        compiler_params=_cparams(("parallel", "arbitrary")),
        name="rwkv_prompt",
    )(proj, *consts)


def _unpack_rwkv_state(sp):
    nb = sp.shape[0]
    s = sp.reshape(nb, RW_HEADS // 2, RW_HEAD, 2, RW_HEAD)
    return jnp.transpose(s, (0, 1, 3, 2, 4)).reshape(nb, RW_HEADS, RW_HEAD, RW_HEAD)


def _rwkv_s_prep_kernel(p_ref, prev_ref, mu_ref, w0_ref, w2_ref, a0_ref, a2_ref, g2_ref, kk_ref, ka_ref, rk_ref,
                        o_ref):
    r, k2, v, lw, a, g, kkraw = _rwkv_tokens(p_ref[...], prev_ref[...], mu_ref[...], w0_ref[...], w2_ref[...],
                                             a0_ref[...], a2_ref[...], g2_ref[...], kk_ref[...], ka_ref[...])
    n = r.shape[0]
    m0, _, _ = _pair_masks(n)
    rk = rk_ref[...]
    o_ref[0] = r
    o_ref[1] = jnp.exp(lw)
    o_ref[2] = k2
    o_ref[3] = v
    o_ref[6] = g
    for pi in range(RW_HEADS // 2):
        sl = slice(pi * LANES, (pi + 1) * LANES)
        kkp = kkraw[:, sl]
        kkn = kkp / jnp.maximum(jnp.sqrt(_headsum_pair(kkp * kkp, m0)), 1e-12)
        o_ref[4, :, sl] = -kkn
        o_ref[5, :, sl] = kkn * a[:, sl]
        o_ref[7, :, sl] = _headsum_pair(r[:, sl] * k2[:, sl] * rk[:, sl], m0) * v[:, sl]


def _rwkv_s_kernel(vec_ref, s_ref, lnw_ref, lnb_ref, so_ref, y_ref):
    nbb = s_ref.shape[1]
    n = RW_HEAD
    eye = lax.broadcasted_iota(jnp.int32, (n, n), 0) == lax.broadcasted_iota(jnp.int32, (n, n), 1)

    def body(bi, carry):
        for h in range(RW_HEADS):
            hs = pl.ds(h, 1)
            s = s_ref[0, bi, h]
            r_ = vec_ref[0, bi, hs, :]
            w_ = vec_ref[1, bi, hs, :]
            k_ = vec_ref[2, bi, hs, :]
            v_ = vec_ref[3, bi, hs, :]
            a_ = vec_ref[4, bi, hs, :]
            b_ = vec_ref[5, bi, hs, :]
            g_ = vec_ref[6, bi, hs, :]
            bonus_ = vec_ref[7, bi, hs, :]
            sa = jnp.sum(s * a_, axis=-1, keepdims=True)
            vc = jnp.sum(jnp.where(eye, v_, 0.0), axis=-1, keepdims=True)
            sn = s * w_ + sa * b_ + vc * k_
            so_ref[bi, h] = sn
            yc = jnp.sum(sn * r_, axis=-1, keepdims=True)
            yr = jnp.sum(jnp.where(eye, yc, 0.0), axis=0, keepdims=True)
            mean = jnp.mean(yr, axis=-1, keepdims=True)
            d = yr - mean
            var = jnp.mean(d * d, axis=-1, keepdims=True)
            yn = d * lax.rsqrt(var + RW_GN_EPS) * lnw_ref[hs, :] + lnb_ref[hs, :]
            y_ref[bi, hs, :] = (yn + bonus_) * g_
        return carry

    lax.fori_loop(0, nbb, body, 0)


def _rwkv_sample(proj, row0_blk, shift_prev, state, layer, wts, lnw, lnb):
    nb = shift_prev.shape[0]
    consts = wts
    vec = pl.pallas_call(
        _rwkv_s_prep_kernel,
        grid=(1,),
        in_specs=[pl.BlockSpec((nb, RW_PAD), lambda i: (row0_blk, 0)),
                  pl.BlockSpec((nb, RW_PAD), lambda i: (0, 0))] + [_const_spec(x.shape) for x in consts],
        out_specs=pl.BlockSpec((8, nb, BRANCH_W), lambda i: (0, 0, 0)),
        out_shape=jax.ShapeDtypeStruct((8, nb, BRANCH_W), F32),
        compiler_params=_cparams(("arbitrary",)),
        name="rwkv_sample_prep",
    )(proj, shift_prev, *consts)
    vec = vec.reshape(8, nb, RW_HEADS, RW_HEAD)
    bb = 8
    s_new, y = pl.pallas_call(
        _rwkv_s_kernel,
        grid=(nb // bb,),
        in_specs=[pl.BlockSpec((8, bb, RW_HEADS, RW_HEAD), lambda i: (0, i, 0, 0)),
                  pl.BlockSpec((1, bb, RW_HEADS, RW_HEAD, RW_HEAD), lambda i: (layer, i, 0, 0, 0)),
                  _const_spec((RW_HEADS, RW_HEAD)), _const_spec((RW_HEADS, RW_HEAD))],
        out_specs=[pl.BlockSpec((bb, RW_HEADS, RW_HEAD, RW_HEAD), lambda i: (i, 0, 0, 0)),
                   pl.BlockSpec((bb, RW_HEADS, RW_HEAD), lambda i: (i, 0, 0))],
        out_shape=[jax.ShapeDtypeStruct((nb, RW_HEADS, RW_HEAD, RW_HEAD), F32),
                   jax.ShapeDtypeStruct((nb, RW_HEADS, RW_HEAD), F32)],
        compiler_params=_cparams(("parallel",)),
        name="rwkv_sample",
    )(vec, state, lnw.reshape(RW_HEADS, RW_HEAD), lnb.reshape(RW_HEADS, RW_HEAD))
    return y.reshape(nb, BRANCH_W), s_new


def _expand_heads():
    k = lax.broadcasted_iota(jnp.int32, (LANES, BRANCH_W), 0)
    c = lax.broadcasted_iota(jnp.int32, (LANES, BRANCH_W), 1)
    return jnp.where(jnp.right_shift(c, 6) == k, 1.0, 0.0).astype(BF16)


def _ssd_p_kernel(p_ref, cw_ref, cb_ref, dtb_ref, alog_ref, dsk_ref, nw_ref, y_ref, st_ref, buf_scr):
    cidx = pl.program_id(1)
    n = p_ref.shape[0]

    @pl.when(cidx == 0)
    def _():
        st_ref[...] = jnp.zeros_like(st_ref)
        buf_scr[0:8, :] = jnp.zeros((8, SSD_CONV_DIM), F32)

    z = p_ref[:, 0:BRANCH_W]
    buf_scr[8:8 + n, :] = p_ref[:, BRANCH_W:BRANCH_W + SSD_CONV_DIM]
    conv = cb_ref[...]
    for i in range(SSD_CONV):
        conv = conv + cw_ref[i:i + 1, :] * buf_scr[pl.ds(8 - (SSD_CONV - 1) + i, n), :]
    buf_scr[0:8, :] = buf_scr[n:n + 8, :]
    xa = _silu(conv)
    xs = xa[:, 0:BRANCH_W]
    bm = xa[:, BRANCH_W:BRANCH_W + SSD_GROUPS * SSD_STATE]
    cm = xa[:, BRANCH_W + SSD_GROUPS * SSD_STATE:]
    dt = _softplus(p_ref[:, BRANCH_W + SSD_CONV_DIM:SSD_PAD] + dtb_ref[...])
    dte = _dot_sel_r(dt, _expand_heads())
    da = dte * (-jnp.exp(alog_ref[...]))
    cum = _dot_sel_l(_tri_incl(n), da)
    xdt = xs * dte
    ecum = jnp.exp(cum)
    cl = cum[n - 1:n, :]
    xdl = xdt * jnp.exp(cl - cum)
    pl_ = jnp.exp(cl)

    m0, row, col = _pair_masks(n)
    incl = col <= row
    eye = col == row
    gw = BRANCH_W // SSD_GROUPS
    ppg = gw // LANES
    ys = []
    for gi in range(SSD_GROUPS):
        gs = slice(gi * gw, (gi + 1) * gw)
        bg = bm[:, gi * SSD_STATE:(gi + 1) * SSD_STATE]
        cg = cm[:, gi * SSD_STATE:(gi + 1) * SSD_STATE]
        cbp = _dgb(cg, jnp.concatenate([bg, bg], axis=0), NT)
        st = st_ref[0, :, gs]
        cs = _dgb(cg, st)
        for q in range(ppg):
            sl = slice(gi * gw + q * LANES, gi * gw + (q + 1) * LANES)
            cp = cum[:, sl]
            rp = jnp.sum(jnp.where(eye, cp, 0.0), axis=0, keepdims=True)
            seg = jnp.exp(jnp.where(incl, cp - rp, -jnp.inf))
            yp = _dgb(cbp * seg, _bd(xdt[:, sl], m0)) + ecum[:, sl] * cs[:, q * LANES:(q + 1) * LANES]
            ys.append(yp)
        st_ref[0, :, gs] = st * pl_[:, gs] + _dgb(bg, xdl[:, gs], TN)
    y = jnp.concatenate(ys, axis=1) + xs * dsk_ref[...]
    y = y * _silu(z)
    for gi in range(SSD_GROUPS):
        gs = slice(gi * gw, (gi + 1) * gw)
        yg = y[:, gs]
        yg = yg * lax.rsqrt(jnp.mean(yg * yg, axis=-1, keepdims=True) + GROUP_EPS)
        y_ref[:, gs] = (yg * nw_ref[:, gs]).astype(y_ref.dtype)


def _ssd_consts(conv_w, conv_b, dt_bias, a_log, d_skip, norm_w):
    dtb = jnp.pad(dt_bias, (0, LANES - SSD_HEADS)).reshape(1, LANES)
    return (conv_w, _row(conv_b), dtb, _row(jnp.repeat(a_log, SSD_HEAD)), _row(jnp.repeat(d_skip, SSD_HEAD)),
            _row(norm_w))


def _ssd_prompt(proj, nb, t, consts):
    nc = t // CHUNK
    return pl.pallas_call(
        _ssd_p_kernel,
        grid=(nb, nc),
        in_specs=[pl.BlockSpec((CHUNK, SSD_PAD), lambda b, c: (b * nc + c, 0))]
        + [_const_spec(x.shape) for x in consts],
        out_specs=[pl.BlockSpec((CHUNK, BRANCH_W), lambda b, c: (b * nc + c, 0)),
                   pl.BlockSpec((1, SSD_STATE, BRANCH_W), lambda b, c: (b, 0, 0))],
        out_shape=[jax.ShapeDtypeStruct((nb * t, BRANCH_W), BF16),
                   jax.ShapeDtypeStruct((nb, SSD_STATE, BRANCH_W), F32)],
        scratch_shapes=[pltpu.VMEM((CHUNK + 8, SSD_CONV_DIM), F32)],
        compiler_params=_cparams(("parallel", "arbitrary")),
        name="ssd_prompt",
    )(proj, *consts)


def _unpack_ssd_state(st):
    nb = st.shape[0]
    return jnp.transpose(st.reshape(nb, SSD_STATE, SSD_HEADS, SSD_HEAD), (0, 2, 3, 1))


def _ssd_s_prep_kernel(p_ref, cv_ref, cw_ref, cb_ref, dtb_ref, alog_ref, dsk_ref, o_ref, bc_ref):
    z = p_ref[:, 0:BRANCH_W]
    conv = cb_ref[...] + cw_ref[SSD_CONV - 1:SSD_CONV, :] * p_ref[:, BRANCH_W:BRANCH_W + SSD_CONV_DIM]
    for i in range(SSD_CONV - 1):
        conv = conv + cw_ref[i:i + 1, :] * cv_ref[i]
    xa = _silu(conv)
    xs = xa[:, 0:BRANCH_W]
    dt = _softplus(p_ref[:, BRANCH_W + SSD_CONV_DIM:SSD_PAD] + dtb_ref[...])
    dte = _dot_sel_r(dt, _expand_heads())
    o_ref[0] = xs * dte
    o_ref[1] = jnp.exp(dte * (-jnp.exp(alog_ref[...])))
    o_ref[2] = xs * dsk_ref[...]
    o_ref[3] = _silu(z)
    bc_ref[...] = xa[:, BRANCH_W:]


def _ssd_s_kernel(vec_ref, bc_ref, s_ref, nw_ref, so_ref, y_ref):
    nbb = s_ref.shape[1]
    n = SSD_HEAD
    eye = lax.broadcasted_iota(jnp.int32, (n, n), 0) == lax.broadcasted_iota(jnp.int32, (n, n), 1)
    hpg = SSD_HEADS // SSD_GROUPS

    def body(bi, carry):
        for h in range(SSD_HEADS):
            hs = pl.ds(h, 1)
            gi = h // hpg
            s = s_ref[0, bi, h]
            xdt_ = vec_ref[0, bi, hs, :]
            dec = vec_ref[1, bi, hs, 0:1]
            b_ = bc_ref[bi, pl.ds(gi, 1), :]
            c_ = bc_ref[bi, pl.ds(SSD_GROUPS + gi, 1), :]
            xc = jnp.sum(jnp.where(eye, xdt_, 0.0), axis=-1, keepdims=True)
            sn = s * dec + xc * b_
            so_ref[bi, h] = sn
            yc = jnp.sum(sn * c_, axis=-1, keepdims=True)
            yr = jnp.sum(jnp.where(eye, yc, 0.0), axis=0, keepdims=True)
            y_ref[bi, hs, :] = (yr + vec_ref[2, bi, hs, :]) * vec_ref[3, bi, hs, :]
        for gi in range(SSD_GROUPS):
            rs = pl.ds(gi * hpg, hpg)
            yg = y_ref[bi, rs, :]
            ms = jnp.sum(jnp.sum(yg * yg, axis=-1, keepdims=True), axis=0, keepdims=True) * (1.0 / (hpg * n))
            y_ref[bi, rs, :] = yg * lax.rsqrt(ms + GROUP_EPS) * nw_ref[rs, :]
        return carry

    lax.fori_loop(0, nbb, body, 0)


def _ssd_sample(proj, row0_blk, conv_prev, state, layer, consts):
    nb = conv_prev.shape[1]
    cw, cb, dtb, alog, dsk, nw = consts
    vec, bc = pl.pallas_call(
        _ssd_s_prep_kernel,
        grid=(1,),
        in_specs=[pl.BlockSpec((nb, SSD_PAD), lambda i: (row0_blk, 0)),
                  _const_spec(conv_prev.shape)] + [_const_spec(x.shape) for x in (cw, cb, dtb, alog, dsk)],
        out_specs=[_const_spec((4, nb, BRANCH_W)), _const_spec((nb, 2 * SSD_GROUPS * SSD_STATE))],
        out_shape=[jax.ShapeDtypeStruct((4, nb, BRANCH_W), F32),
                   jax.ShapeDtypeStruct((nb, 2 * SSD_GROUPS * SSD_STATE), F32)],
        compiler_params=_cparams(("arbitrary",)),
        name="ssd_sample_prep",
    )(proj, conv_prev, cw, cb, dtb, alog, dsk)
    vec = vec.reshape(4, nb, SSD_HEADS, SSD_HEAD)
    bc = bc.reshape(nb, 2 * SSD_GROUPS, SSD_STATE)
    bb = 8
    s_new, y = pl.pallas_call(
        _ssd_s_kernel,
        grid=(nb // bb,),
        in_specs=[pl.BlockSpec((4, bb, SSD_HEADS, SSD_HEAD), lambda i: (0, i, 0, 0)),
                  pl.BlockSpec((bb, 2 * SSD_GROUPS, SSD_STATE), lambda i: (i, 0, 0)),
                  pl.BlockSpec((1, bb, SSD_HEADS, SSD_HEAD, SSD_STATE), lambda i: (layer, i, 0, 0, 0)),
                  _const_spec((SSD_HEADS, SSD_HEAD))],
        out_specs=[pl.BlockSpec((bb, SSD_HEADS, SSD_HEAD, SSD_STATE), lambda i: (i, 0, 0, 0)),
                   pl.BlockSpec((bb, SSD_HEADS, SSD_HEAD), lambda i: (i, 0, 0))],
        out_shape=[jax.ShapeDtypeStruct((nb, SSD_HEADS, SSD_HEAD, SSD_STATE), F32),
                   jax.ShapeDtypeStruct((nb, SSD_HEADS, SSD_HEAD), F32)],
        compiler_params=_cparams(("parallel",)),
        name="ssd_sample",
    )(vec, bc, state, nw.reshape(SSD_HEADS, SSD_HEAD))
    return y.reshape(nb, BRANCH_W), s_new


def _gla_tokens(p, au, ab):
    q = p[:, 0:GLA_DK] * (GLA_HK ** -0.5)
    k = p[:, GLA_DK:2 * GLA_DK]
    v = p[:, 2 * GLA_DK:2 * GLA_DK + GLA_DV]
    r = p[:, 2 * GLA_DK + GLA_DV:2 * GLA_DK + 2 * GLA_DV]
    ad = p[:, 2 * GLA_DK + 2 * GLA_DV:GLA_PAD]
    lg = -_softplus(-(_dgb(ad, au) + ab)) * (1.0 / GLA_TAU)
    return q, k, v, r, lg


def _gla_p_kernel(p_ref, au_ref, ab_ref, nw_ref, o_ref, st_ref):
    cidx = pl.program_id(1)

    @pl.when(cidx == 0)
    def _():
        st_ref[...] = jnp.zeros_like(st_ref)

    n = p_ref.shape[0]
    q, k, v, r, lg = _gla_tokens(p_ref[...], au_ref[...], ab_ref[...])
    cum = _dot_sel_l(_tri_incl(n), lg)
    cumx = cum - lg
    cl = cum[n - 1:n, :]
    qe = q * jnp.exp(cum)
    kb = k * jnp.exp(cl - cum)
    pl_ = jnp.exp(cl)
    rowi = lax.broadcasted_iota(jnp.int32, (SUB, 1), 0)
    for h in range(GLA_HEADS):
        sk = slice(h * GLA_HK, (h + 1) * GLA_HK)
        sv = slice(h * GLA_HV, (h + 1) * GLA_HV)
        st = st_ref[0, h]
        vh = v[:, sv]
        parts = []
        for blk in range(n // SUB):
            lo = blk * SUB
            rs = slice(lo, lo + SUB)
            q_i = q[rs, sk]
            c_i = cum[rs, sk]
            k_i = k[rs, sk]
            v_i = vh[rs]
            if blk > 0:
                cref = cumx[lo:lo + 1, sk]
                qt = q_i * jnp.exp(c_i - cref)
                kt = k[0:lo, sk] * jnp.exp(cref - cum[0:lo, sk])
                acc = _dgb(_dgb(qt, kt, NT), vh[0:lo])
            else:
                acc = jnp.zeros((SUB, GLA_HV), F32)
            for s in range(SUB):
                e = jnp.exp(c_i - c_i[s:s + 1])
                w = jnp.sum(q_i * k_i[s:s + 1] * e, axis=-1, keepdims=True)
                acc = acc + jnp.where(rowi >= s, w, 0.0) * v_i[s:s + 1]
            parts.append(acc)
        o = jnp.concatenate(parts, axis=0) + _dgb(qe[:, sk], st, NT)
        st_ref[0, h] = st * pl_[:, sk] + _dgb(vh, kb[:, sk], TN)
        o = o * lax.rsqrt(jnp.mean(o * o, axis=-1, keepdims=True) + GROUP_EPS) * nw_ref[...]
        o_ref[:, sv] = (o * _silu(r[:, sv])).astype(o_ref.dtype)


def _gla_consts(alpha_up, alpha_b, norm_w):
    au = jnp.pad(alpha_up, ((0, LANES - GLA_LORA), (0, 0))).astype(BF16)
    return au, _row(alpha_b), _row(norm_w)


def _gla_prompt(proj, nb, t, consts):
    nc = t // CHUNK
    return pl.pallas_call(
        _gla_p_kernel,
        grid=(nb, nc),
        in_specs=[pl.BlockSpec((CHUNK, GLA_PAD), lambda b, c: (b * nc + c, 0))]
        + [_const_spec(x.shape) for x in consts],
        out_specs=[pl.BlockSpec((CHUNK, GLA_DV), lambda b, c: (b * nc + c, 0)),
                   pl.BlockSpec((1, GLA_HEADS, GLA_HV, GLA_HK), lambda b, c: (b, 0, 0, 0))],
        out_shape=[jax.ShapeDtypeStruct((nb * t, GLA_DV), BF16),
                   jax.ShapeDtypeStruct((nb, GLA_HEADS, GLA_HV, GLA_HK), F32)],
        compiler_params=_cparams(("parallel", "arbitrary")),
        name="gla_prompt",
    )(proj, *consts)


def _gla_s_prep_kernel(p_ref, au_ref, ab_ref, qk_ref, vr_ref):
    q, k, v, r, lg = _gla_tokens(p_ref[...], au_ref[...], ab_ref[...])
    qk_ref[0] = q
    qk_ref[1] = k
    qk_ref[2] = jnp.exp(lg)
    vr_ref[0] = v
    vr_ref[1] = _silu(r)


def _gla_s_kernel(qk_ref, vr_ref, s_ref, nw_ref, so_ref, y_ref):
    nbb = s_ref.shape[1]
    n = GLA_HK
    eye = lax.broadcasted_iota(jnp.int32, (n, n), 0) == lax.broadcasted_iota(jnp.int32, (n, n), 1)

    def col(x):
        return jnp.sum(jnp.where(eye, x, 0.0), axis=-1, keepdims=True)

    def body(bi, carry):
        for h in range(GLA_HEADS):
            hs = pl.ds(h, 1)
            s = s_ref[0, bi, h]
            sn = s * col(qk_ref[2, bi, hs, :]) + col(qk_ref[1, bi, hs, :]) * vr_ref[0, bi, hs, :]
            so_ref[bi, h] = sn
            o = jnp.sum(sn * col(qk_ref[0, bi, hs, :]), axis=0, keepdims=True)
            o = o * lax.rsqrt(jnp.mean(o * o, axis=-1, keepdims=True) + GROUP_EPS) * nw_ref[...]
            y_ref[bi, hs, :] = o * vr_ref[1, bi, hs, :]
        return carry

    lax.fori_loop(0, nbb, body, 0)


def _gla_sample(proj, row0_blk, nb, state, layer, consts):
    au, ab, nw = consts
    qk, vr = pl.pallas_call(
        _gla_s_prep_kernel,
        grid=(1,),
        in_specs=[pl.BlockSpec((nb, GLA_PAD), lambda i: (row0_blk, 0)), _const_spec(au.shape),
                  _const_spec(ab.shape)],
        out_specs=[_const_spec((3, nb, GLA_DK)), _const_spec((2, nb, GLA_DV))],
        out_shape=[jax.ShapeDtypeStruct((3, nb, GLA_DK), F32), jax.ShapeDtypeStruct((2, nb, GLA_DV), F32)],
        compiler_params=_cparams(("arbitrary",)),
        name="gla_sample_prep",
    )(proj, au, ab)
    qk = qk.reshape(3, nb, GLA_HEADS, GLA_HK)
    vr = vr.reshape(2, nb, GLA_HEADS, GLA_HV)
    bb = 8
    s_new, y = pl.pallas_call(
        _gla_s_kernel,
        grid=(nb // bb,),
        in_specs=[pl.BlockSpec((3, bb, GLA_HEADS, GLA_HK), lambda i: (0, i, 0, 0)),
                  pl.BlockSpec((2, bb, GLA_HEADS, GLA_HV), lambda i: (0, i, 0, 0)),
                  pl.BlockSpec((1, bb, GLA_HEADS, GLA_HK, GLA_HV), lambda i: (layer, i, 0, 0, 0)),
                  _const_spec(nw.shape)],
        out_specs=[pl.BlockSpec((bb, GLA_HEADS, GLA_HK, GLA_HV), lambda i: (i, 0, 0, 0)),
                   pl.BlockSpec((bb, GLA_HEADS, GLA_HV), lambda i: (i, 0, 0))],
        out_shape=[jax.ShapeDtypeStruct((nb, GLA_HEADS, GLA_HK, GLA_HV), F32),
                   jax.ShapeDtypeStruct((nb, GLA_HEADS, GLA_HV), F32)],
        compiler_params=_cparams(("parallel",)),
        name="gla_sample",
    )(qk, vr, state, nw)
    return y.reshape(nb, GLA_DV), s_new


def kernel(x_prompt, x_sample, state_rwkv, state_rwkv_shift, state_ssd, state_ssd_conv, state_gla, cache_mem_k, cache_mem_v, mem_prompt, norm_mix_pre, norm_mix_post, norm_x_pre, norm_x_post, norm_ffn_pre, norm_ffn_post, w_in, rw_mu, rw_w0, rw_w2, rw_a0, rw_a2, rw_g2, rw_kk, rw_ka, rw_rk, rw_ln_w, rw_ln_b, ssd_conv_w, ssd_conv_b, ssd_dt_bias, ssd_a_log, ssd_d, ssd_norm_w, gla_alpha_up, gla_alpha_b, gla_norm_w, w_branch, w_out, x_mem_norm, x_wq, x_wk, x_wv, x_wo, ffn_up, ffn_down):
    nbp, t, d = x_prompt.shape
    nbs = x_sample.shape[0]
    depth = w_in.shape[0]
    mp = nbp * t
    m = mp + nbs
    sblk = mp // nbs

    x = jnp.concatenate([x_prompt.reshape(mp, d), x_sample.reshape(nbs, d)], axis=0)
    mem_rows = mem_prompt.reshape(nbp * N_MEM, d)
    tm_norm = _row_tile(m, (1040, 1024, 512, 256, 128, 64, 8))

    w_rw_all, w_ssd_all, w_gla_all, w_gate_all = _split_w_in(w_in)
    wb_all = _cast_bf16(w_branch.reshape(depth * N_BRANCH, BRANCH_W, d)).reshape(depth, N_BRANCH, BRANCH_W, d)
    w_out_b, wq_b, wk_b, wv_b, wo_b, up_b, down_b = [_cast_bf16(w) for w in
                                                     (w_out, x_wq, x_wk, x_wv, x_wo, ffn_up, ffn_down)]

    xn = _norm_rows(x, norm_mix_pre[0], tm_norm)
    p_acc = [[] for _ in range(7)]
    s_acc = [[] for _ in range(5)]
    for l in range(depth):
        proj_rw = _mm(xn, w_rw_all, l, name="proj_rwkv")
        proj_ssd = _mm(xn, w_ssd_all, l, name="proj_ssd")
        proj_gla = _mm(xn, w_gla_all, l, name="proj_gla")
        gate = _mm(xn, w_gate_all, l, name="proj_gate")

        w2p, a2p, g2p = _rwkv_weights(rw_w2[l], rw_a2[l], rw_g2[l])
        rw_tok = (_row(jnp.pad(rw_mu[l], (0, RW_PAD - RW_PROJ))), _row(rw_w0[l]), w2p, _row(rw_a0[l]), a2p, g2p,
                  _row(rw_kk[l]), _row(rw_ka[l]), _row(rw_rk[l]))
        ya_p, rw_sp = _rwkv_prompt(proj_rw, nbp, t, rw_tok + (_row(rw_ln_w[l]), _row(rw_ln_b[l])))
        shift_prev = jnp.pad(state_rwkv_shift[l], ((0, 0), (0, RW_PAD - RW_PROJ)))
        ya_s, rw_ss = _rwkv_sample(proj_rw, sblk, shift_prev, state_rwkv, l, rw_tok, rw_ln_w[l], rw_ln_b[l])

        ssd_c = _ssd_consts(ssd_conv_w[l], ssd_conv_b[l], ssd_dt_bias[l], ssd_a_log[l], ssd_d[l], ssd_norm_w[l])
        yb_p, ssd_sp = _ssd_prompt(proj_ssd, nbp, t, ssd_c)
        conv_prev = jnp.transpose(state_ssd_conv[l], (1, 0, 2))
        yb_s, ssd_ss = _ssd_sample(proj_ssd, sblk, conv_prev, state_ssd, l, ssd_c)

        gla_c = _gla_consts(gla_alpha_up[l], gla_alpha_b[l], gla_norm_w[l])
        yc_p, gla_sp = _gla_prompt(proj_gla, nbp, t, gla_c)
        yc_s, gla_ss = _gla_sample(proj_gla, sblk, nbs, state_gla, l, gla_c)

        y3 = jnp.stack([jnp.concatenate([ya_p, ya_s.astype(BF16)], axis=0),
                        jnp.concatenate([yb_p, yb_s.astype(BF16)], axis=0),
                        jnp.concatenate([yc_p, yc_s.astype(BF16)], axis=0)])
        mix = _merge(y3, wb_all, l, gate)
        x, xn = _mm_resnorm(mix, w_out_b, l, x, norm_mix_post[l], norm_x_pre[l], name="out_proj")

        mn = _norm_rows(mem_rows, x_mem_norm[l], _row_tile(mem_rows.shape[0], (1024, 512, 256, 128, 64, 8)))
        mk = _mm(mn, wk_b, l, name="mem_k")
        mv = _mm(mn, wv_b, l, name="mem_v")
        q = _mm(xn, wq_b, l, out_dtype=BF16, name="attn_q")
        o_p = _attn_prompt(q, mk.reshape(nbp, N_MEM, d), mv.reshape(nbp, N_MEM, d), nbp, t)
        o_s = _attn_sample(q[mp:].reshape(nbs, X_HEADS, X_HEAD), cache_mem_k, cache_mem_v, l)
        o = jnp.concatenate([o_p, o_s.reshape(nbs, d)], axis=0)
        x, xn = _mm_resnorm(o, wo_b, l, x, norm_x_post[l], norm_ffn_pre[l], name="attn_out")

        hf = _mm(xn, up_b, l, out_dtype=BF16, act="relu2", name="ffn_up")
        g_next = norm_mix_pre[l + 1] if l + 1 < depth else norm_mix_pre[l]
        x, xn = _mm_resnorm(hf, down_b, l, x, norm_ffn_post[l], g_next, name="ffn_down")

        last = [b * t + t - 1 for b in range(nbp)]
        rw_shift_p = jnp.stack([proj_rw[i, :RW_PROJ] for i in last])
        ssd_conv_p = jnp.stack([proj_ssd[i - (SSD_CONV - 2):i + 1, BRANCH_W:BRANCH_W + SSD_CONV_DIM] for i in last])
        new_p = (_unpack_rwkv_state(rw_sp), rw_shift_p, _unpack_ssd_state(ssd_sp), ssd_conv_p,
                 jnp.transpose(gla_sp, (0, 1, 3, 2)),
                 mk.reshape(nbp, N_MEM, X_HEADS, X_HEAD), mv.reshape(nbp, N_MEM, X_HEADS, X_HEAD))
        xbc_s = proj_ssd[mp:, BRANCH_W:BRANCH_W + SSD_CONV_DIM]
        new_s = (rw_ss, proj_rw[mp:, :RW_PROJ], ssd_ss,
                 jnp.concatenate([state_ssd_conv[l][:, 1:], xbc_s[:, None, :]], axis=1), gla_ss)
        for acc, val in zip(p_acc, new_p):
            acc.append(val)
        for acc, val in zip(s_acc, new_s):
            acc.append(val)

    outs_p = [jnp.stack(a) for a in p_acc]
    outs_s = [jnp.stack(a) for a in s_acc]
    return (x[:mp].reshape(nbp, t, d), x[mp:].reshape(nbs, 1, d), *outs_p, *outs_s)
```

```python
import functools

import jax
import jax.numpy as jnp
from jax import lax
from jax.experimental import pallas as pl
from jax.experimental.pallas import tpu as pltpu

F32 = jnp.float32
BF16 = jnp.bfloat16

D_MODEL = 2048
BRANCH_W = 1024
N_BRANCH = 3
RW_HEADS = 16
RW_HEAD = 64
RW_PROJ = 3360
RW_PAD = 3456
RW_GN_EPS = 64e-5
SSD_HEADS = 16
SSD_HEAD = 64
SSD_GROUPS = 2
SSD_STATE = 128
SSD_CONV = 4
SSD_CONV_DIM = 1536
SSD_PROJ = 2576
SSD_PAD = 2688
GLA_HEADS = 4
GLA_DK = 512
GLA_DV = 1024
GLA_HK = 128
GLA_HV = 256
GLA_LORA = 16
GLA_TAU = 16.0
GLA_PROJ = 3088
GLA_PAD = 3200
N_MEM = 256
X_HEADS = 4
X_HEAD = 512
D_FF = 8192
NORM_EPS = 1e-6
GROUP_EPS = 1e-5
CHUNK = 64
SUB = 16
LANES = 128
VMEM_LIMIT = 56 * 1024 * 1024

NN = ((1,), (0,))
NT = ((1,), (1,))
TN = ((0,), (0,))


def _cparams(sem):
    return pltpu.CompilerParams(dimension_semantics=sem, vmem_limit_bytes=VMEM_LIMIT)


def _dg(a, b, dims=NN):
    return lax.dot_general(a, b, (dims, ((), ())), preferred_element_type=F32)


def _dgb(a, b, dims=NN):
    return _dg(a.astype(BF16), b.astype(BF16), dims)


def _hl(x):
    h = x.astype(BF16)
    return h, (x - h.astype(F32)).astype(BF16)


def _dot_hi(a, b, dims=NN):
    ah, al = _hl(a)
    bh, bl = _hl(b)
    return _dg(ah, bh, dims) + (_dg(ah, bl, dims) + _dg(al, bh, dims))


def _split3(x):
    h = x.astype(BF16)
    r = x - h.astype(F32)
    m = r.astype(BF16)
    return h, m, (r - m.astype(F32)).astype(BF16)


def _dot_sel_l(sel, x):
    h, m, l = _split3(x)
    return _dg(sel, h) + (_dg(sel, m) + _dg(sel, l))


def _dot_sel_r(x, sel):
    h, m, l = _split3(x)
    return _dg(h, sel) + (_dg(m, sel) + _dg(l, sel))


def _softplus(x):
    return jnp.maximum(x, 0.0) + jnp.log1p(jnp.exp(-jnp.abs(x)))


def _sigmoid(x):
    return 1.0 / (1.0 + jnp.exp(-x))


def _silu(x):
    return x * _sigmoid(x)


def _tri_incl(n):
    r = lax.broadcasted_iota(jnp.int32, (n, n), 0)
    c = lax.broadcasted_iota(jnp.int32, (n, n), 1)
    return jnp.where(c <= r, 1.0, 0.0).astype(BF16)


def _pair_masks(rows):
    lane = lax.broadcasted_iota(jnp.int32, (rows, LANES), 1)
    row = lax.broadcasted_iota(jnp.int32, (rows, LANES), 0)
    m0 = lane < 64
    col = jnp.bitwise_and(lane, 63)
    return m0, row, col


def _headsum_pair(x, m0):
    s0 = jnp.sum(jnp.where(m0, x, 0.0), axis=-1, keepdims=True)
    s1 = jnp.sum(jnp.where(m0, 0.0, x), axis=-1, keepdims=True)
    return jnp.where(m0, s0, s1)


def _bd(x, m0):
    return jnp.concatenate([jnp.where(m0, x, 0.0), jnp.where(m0, 0.0, x)], axis=0)


def _norm_kernel(x_ref, g_ref, o_ref):
    x = x_ref[...]
    y = x * lax.rsqrt(jnp.mean(x * x, axis=-1, keepdims=True) + NORM_EPS)
    o_ref[...] = (y * g_ref[...]).astype(o_ref.dtype)


def _norm_rows(x, g, tm):
    m, d = x.shape
    return pl.pallas_call(
        _norm_kernel,
        grid=(m // tm,),
        in_specs=[pl.BlockSpec((tm, d), lambda i: (i, 0)), pl.BlockSpec((1, d), lambda i: (0, 0))],
        out_specs=pl.BlockSpec((tm, d), lambda i: (i, 0)),
        out_shape=jax.ShapeDtypeStruct((m, d), BF16),
        compiler_params=_cparams(("parallel",)),
        name="rmsnorm",
    )(x, g.reshape(1, d))


def _mm_kernel(a_ref, w_ref, o_ref, *scratch, nk, act):
    part = _dg(a_ref[...], w_ref[0])

    def finish(acc):
        if act == "relu2":
            acc = jnp.square(jnp.maximum(acc, 0.0))
        o_ref[...] = acc.astype(o_ref.dtype)

    if nk == 1:
        finish(part)
        return
    acc_ref, = scratch
    k = pl.program_id(2)

    @pl.when(k == 0)
    def _():
        acc_ref[...] = part

    @pl.when(k > 0)
    def _():
        acc_ref[...] += part

    @pl.when(k == nk - 1)
    def _():
        finish(acc_ref[...])


def _mm_resnorm_kernel(a_ref, w_ref, res_ref, gp_ref, gn_ref, x_ref, xn_ref, *scratch, nk):
    part = _dg(a_ref[...], w_ref[0])

    def finish(acc):
        y = acc * lax.rsqrt(jnp.mean(acc * acc, axis=-1, keepdims=True) + NORM_EPS) * gp_ref[...]
        x = res_ref[...] + y
        x_ref[...] = x
        xn = x * lax.rsqrt(jnp.mean(x * x, axis=-1, keepdims=True) + NORM_EPS) * gn_ref[...]
        xn_ref[...] = xn.astype(xn_ref.dtype)

    if nk == 1:
        finish(part)
        return
    acc_ref, = scratch
    k = pl.program_id(1)

    @pl.when(k == 0)
    def _():
        acc_ref[...] = part

    @pl.when(k > 0)
    def _():
        acc_ref[...] += part

    @pl.when(k == nk - 1)
    def _():
        finish(acc_ref[...])


def _pick(n, cands):
    for c in cands:
        if n % c == 0:
            return c
    return n


def _row_tile(m, cands):
    return _pick(m, cands)


def _mm(a, w, layer, out_dtype=F32, act=None, name="mm"):
    m, kd = a.shape
    n = w.shape[2]
    tm = _row_tile(m, (1040, 1024, 512, 256, 128, 64, 8))
    tn = _pick(n, (1024, 1152, 896, 640, 512, 384, 256, 128))
    tk = _pick(kd, (2048, 1024, 512))
    nk = kd // tk
    scratch = [pltpu.VMEM((tm, tn), F32)] if nk > 1 else []
    return pl.pallas_call(
        functools.partial(_mm_kernel, nk=nk, act=act),
        grid=(m // tm, n // tn, nk),
        in_specs=[pl.BlockSpec((tm, tk), lambda i, j, k: (i, k)),
                  pl.BlockSpec((1, tk, tn), lambda i, j, k: (layer, k, j))],
        out_specs=pl.BlockSpec((tm, tn), lambda i, j, k: (i, j)),
        out_shape=jax.ShapeDtypeStruct((m, n), out_dtype),
        scratch_shapes=scratch,
        compiler_params=_cparams(("parallel", "parallel", "arbitrary")),
        name=name,
    )(a, w)


def _mm_resnorm(a, w, layer, res, g_post, g_next, name="mm_resnorm"):
    m, kd = a.shape
    n = w.shape[2]
    if kd <= D_MODEL:
        tm, tk = _row_tile(m, (520, 512, 256, 128, 64, 8)), kd
        res_spec = pl.BlockSpec((tm, n), lambda i, k: (i, 0))
    else:
        tm, tk = _row_tile(m, (1040, 1024, 512, 256, 128, 64, 8)), _pick(kd, (512,))
        res_spec = pl.BlockSpec((tm, n), lambda i, k: (i, 0), pipeline_mode=pl.Buffered(1))
    nk = kd // tk
    scratch = [pltpu.VMEM((tm, n), F32)] if nk > 1 else []
    return pl.pallas_call(
        functools.partial(_mm_resnorm_kernel, nk=nk),
        grid=(m // tm, nk),
        in_specs=[pl.BlockSpec((tm, tk), lambda i, k: (i, k)),
                  pl.BlockSpec((1, tk, n), lambda i, k: (layer, k, 0)),
                  res_spec,
                  pl.BlockSpec((1, n), lambda i, k: (0, 0)),
                  pl.BlockSpec((1, n), lambda i, k: (0, 0))],
        out_specs=[pl.BlockSpec((tm, n), lambda i, k: (i, 0)),
                   pl.BlockSpec((tm, n), lambda i, k: (i, 0))],
        out_shape=[jax.ShapeDtypeStruct((m, n), F32), jax.ShapeDtypeStruct((m, n), BF16)],
        scratch_shapes=scratch,
        compiler_params=_cparams(("parallel", "arbitrary")),
        name=name,
    )(a, w, res, g_post.reshape(1, n), g_next.reshape(1, n))


def _merge_kernel(y_ref, w_ref, g_ref, o_ref, acc_ref):
    n = pl.program_id(2)
    z = _dg(y_ref[0], w_ref[0, 0]) * _sigmoid(g_ref[...])

    @pl.when(n == 0)
    def _():
        acc_ref[...] = z

    @pl.when(n > 0)
    def _():
        acc_ref[...] += z

    @pl.when(n == N_BRANCH - 1)
    def _():
        o_ref[...] = acc_ref[...].astype(o_ref.dtype)


def _merge(y3, wb, layer, gate_logits):
    _, m, bw = y3.shape
    d = wb.shape[3]
    tm = _row_tile(m, (1040, 1024, 512, 256, 128, 64, 8))
    tn = 1024
    nj = d // tn
    return pl.pallas_call(
        _merge_kernel,
        grid=(m // tm, nj, N_BRANCH),
        in_specs=[pl.BlockSpec((1, tm, bw), lambda i, j, n: (n, i, 0)),
                  pl.BlockSpec((1, 1, bw, tn), lambda i, j, n: (layer, n, 0, j)),
                  pl.BlockSpec((tm, tn), lambda i, j, n: (i, n * nj + j))],
        out_specs=pl.BlockSpec((tm, tn), lambda i, j, n: (i, j)),
        out_shape=jax.ShapeDtypeStruct((m, d), BF16),
        scratch_shapes=[pltpu.VMEM((tm, tn), F32)],
        compiler_params=_cparams(("parallel", "parallel", "arbitrary")),
        name="merge",
    )(y3, wb, gate_logits)


def _cast_kernel(w_ref, o_ref):
    o_ref[...] = w_ref[...].astype(o_ref.dtype)


def _cast_bf16(w):
    g, r, c = w.shape
    tr = _pick(r, (256, 128, 64, 8))
    return pl.pallas_call(
        _cast_kernel,
        grid=(g, r // tr),
        in_specs=[pl.BlockSpec((1, tr, c), lambda a, i: (a, i, 0))],
        out_specs=pl.BlockSpec((1, tr, c), lambda a, i: (a, i, 0)),
        out_shape=jax.ShapeDtypeStruct((g, r, c), BF16),
        compiler_params=_cparams(("parallel", "parallel")),
        name="cast_bf16",
    )(w)


_SEGS = ((0, RW_PROJ, RW_PAD),
         (RW_PROJ, SSD_PROJ, SSD_PAD),
         (RW_PROJ + SSD_PROJ, GLA_PROJ, GLA_PAD),
         (RW_PROJ + SSD_PROJ + GLA_PROJ, N_BRANCH * D_MODEL, N_BRANCH * D_MODEL))


def _seg_window(first):
    base = first // LANES * LANES
    return base, first - base


def _split_kernel(*refs):
    ins, outs = refs[:len(_SEGS)], refs[len(_SEGS):]
    for (first, live, padded), w_ref, o_ref in zip(_SEGS, ins, outs):
        _, off = _seg_window(first)
        width = -(-(off + live) // LANES) * LANES
        x = w_ref[0, :, 0:width]
        if off:
            x = pltpu.roll(x, width - off, 1)
        x = x[:, 0:padded]
        if live < padded:
            lane = lax.broadcasted_iota(jnp.int32, x.shape, 1)
            x = jnp.where(lane < live, x, 0.0)
        o_ref[0] = x.astype(o_ref.dtype)


def _split_w_in(w_in):
    depth, d, n_in = w_in.shape
    tr = 128
    in_specs = []
    for first, live, padded in _SEGS:
        base, off = _seg_window(first)
        bw = base if base else padded
        assert off + live <= bw and (base == 0 or base % bw == 0)
        in_specs.append(pl.BlockSpec((1, tr, bw), lambda l, i, b=(base // bw): (l, i, b)))
    return pl.pallas_call(
        _split_kernel,
        grid=(depth, d // tr),
        in_specs=in_specs,
        out_specs=[pl.BlockSpec((1, tr, p), lambda l, i: (l, i, 0)) for _, _, p in _SEGS],
        out_shape=[jax.ShapeDtypeStruct((depth, d, p), BF16) for _, _, p in _SEGS],
        compiler_params=_cparams(("parallel", "parallel")),
        name="split_w_in",
    )(*([w_in] * len(_SEGS)))


def _attn_p_kernel(q_ref, k_ref, v_ref, o_ref):
    s = _dgb(q_ref[...], k_ref[0], NT) * (X_HEAD ** -0.5)
    p = jnp.exp(s - jnp.max(s, axis=-1, keepdims=True))
    attn = p / jnp.sum(p, axis=-1, keepdims=True)
    o_ref[...] = _dgb(attn, v_ref[0]).astype(o_ref.dtype)


def _attn_prompt(q, mem_k, mem_v, nb, t):
    tq = _pick(t, (1024, 512, 256, 128, 64))
    nq = t // tq
    return pl.pallas_call(
        _attn_p_kernel,
        grid=(nb, X_HEADS, nq),
        in_specs=[pl.BlockSpec((tq, X_HEAD), lambda b, h, i: (b * nq + i, h)),
                  pl.BlockSpec((1, N_MEM, X_HEAD), lambda b, h, i: (b, 0, h)),
                  pl.BlockSpec((1, N_MEM, X_HEAD), lambda b, h, i: (b, 0, h))],
        out_specs=pl.BlockSpec((tq, X_HEAD), lambda b, h, i: (b * nq + i, h)),
        out_shape=jax.ShapeDtypeStruct(q.shape, BF16),
        compiler_params=_cparams(("parallel", "parallel", "parallel")),
        name="attn_prompt",
    )(q, mem_k, mem_v)


def _attn_s_kernel(q_ref, k_ref, v_ref, o_ref):
    q = q_ref[0].astype(F32)
    s = jnp.sum(k_ref[0, 0] * q[None], axis=-1, keepdims=True) * (X_HEAD ** -0.5)
    p = jnp.exp(s - jnp.max(s, axis=0, keepdims=True))
    attn = p / jnp.sum(p, axis=0, keepdims=True)
    o_ref[0] = jnp.sum(attn * v_ref[0, 0], axis=0).astype(o_ref.dtype)


def _attn_sample(q3, cache_k, cache_v, layer):
    nb = q3.shape[0]
    kv_spec = pl.BlockSpec((1, 1, N_MEM, X_HEADS, X_HEAD), lambda b: (layer, b, 0, 0, 0))
    return pl.pallas_call(
        _attn_s_kernel,
        grid=(nb,),
        in_specs=[pl.BlockSpec((1, X_HEADS, X_HEAD), lambda b: (b, 0, 0)), kv_spec, kv_spec],
        out_specs=pl.BlockSpec((1, X_HEADS, X_HEAD), lambda b: (b, 0, 0)),
        out_shape=jax.ShapeDtypeStruct((nb, X_HEADS, X_HEAD), BF16),
        compiler_params=_cparams(("parallel",)),
        name="attn_sample",
    )(q3, cache_k, cache_v)


def _rwkv_tokens(p, prev, mu, w0, w2, a0, a2, g2, k_k, k_a):
    c = BRANCH_W
    ps = p + (prev - p) * mu
    r = ps[:, 0:c]
    k = ps[:, c:2 * c]
    v = ps[:, 2 * c:3 * c]
    slab = ps[:, 3 * c:3 * c + 128]
    gslab = ps[:, 3 * c + 128:RW_PAD]
    wl = w0 + _dgb(jnp.tanh(slab), w2)
    lw = -jnp.exp(-_softplus(-wl) - 0.5)
    a = _sigmoid(a0 + _dgb(slab, a2))
    g = _dgb(_sigmoid(gslab), g2)
    kkraw = k * k_k
    k2 = k * (1.0 + (a - 1.0) * k_a)
    return r, k2, v, lw, a, g, kkraw


def _rwkv_p_kernel(p_ref, mu_ref, w0_ref, w2_ref, a0_ref, a2_ref, g2_ref, kk_ref, ka_ref, rk_ref,
                   lnw_ref, lnb_ref, y_ref, s_ref, prev_scr):
    cidx = pl.program_id(1)

    @pl.when(cidx == 0)
    def _():
        s_ref[...] = jnp.zeros_like(s_ref)
        prev_scr[...] = jnp.zeros_like(prev_scr)

    p = p_ref[...]
    n = p.shape[0]
    row1 = lax.broadcasted_iota(jnp.int32, (n, 1), 0)
    prev = jnp.where(row1 == 0, prev_scr[0:1, :], pltpu.roll(p, 1, 0))
    prev_scr[0:1, :] = p[n - 1:n, :]
    r, k2, v, lw, a, g, kkraw = _rwkv_tokens(p, prev, mu_ref[...], w0_ref[...], w2_ref[...], a0_ref[...],
                                             a2_ref[...], g2_ref[...], kk_ref[...], ka_ref[...])
    cum = _dot_sel_l(_tri_incl(n), lw)
    e_c = jnp.exp(cum)
    e_x = jnp.exp(cum - lw)
    e_n = jnp.exp(-cum)
    e_l = jnp.exp(cum[n - 1:n, :] - cum)
    rk = rk_ref[...]
    lnw = lnw_ref[...]
    lnb = lnb_ref[...]

    m0, row, col = _pair_masks(n)
    strict = col < row
    incl = col <= row
    eye = jnp.where(col == row, 1.0, 0.0)
    ms, _, _ = _pair_masks(RW_HEAD)

    prs = range(RW_HEADS // 2)
    sls = [slice(pi * LANES, (pi + 1) * LANES) for pi in prs]
    kkn = []
    for sl in sls:
        kkp = kkraw[:, sl]
        kkn.append(kkp / jnp.maximum(jnp.sqrt(_headsum_pair(kkp * kkp, m0)), 1e-12))
    bv = [kkn[i] * a[:, sls[i]] for i in prs]
    at = [-kkn[i] * e_x[:, sls[i]] for i in prs]
    rt = [r[:, sl] * e_c[:, sl] for sl in sls]
    vv = [v[:, sl] for sl in sls]
    ar = [jnp.concatenate([at[i], rt[i]], axis=0) for i in prs]
    sab = [_dot_hi(ar[i], _bd(bv[i] * e_n[:, sls[i]], m0), NT) for i in prs]
    sak = [_dgb(ar[i], _bd(k2[:, sls[i]] * e_n[:, sls[i]], m0), NT) for i in prs]
    a_ab = [jnp.where(strict, s[:n], 0.0) for s in sab]
    a_ak = [jnp.where(strict, s[:n], 0.0) for s in sak]
    m_rb = [jnp.where(incl, s[n:], 0.0) for s in sab]
    m_rk = [jnp.where(incl, s[n:], 0.0) for s in sak]
    tinv = [eye + x for x in a_ab]
    xs = a_ab
    akv = [_dgb(a_ak[i], _bd(vv[i], m0)) for i in prs]
    for _ in range(n.bit_length() - 2):
        xs = [_dot_hi(x, _bd(x, m0)) for x in xs]
        tinv = [tinv[i] + _dot_hi(tinv[i], _bd(xs[i], m0)) for i in prs]
    tw = [_dgb(tinv[i], jnp.concatenate([_bd(at[i], m0), _bd(akv[i], m0)], axis=1)) for i in prs]
    s0 = [s_ref[0, pi] for pi in prs]
    us = [_dgb(jnp.concatenate([tw[i][:, :LANES], rt[i]], axis=0), _bd(s0[i], ms), NT) for i in prs]
    u = [us[i][:n] + tw[i][:, LANES:] for i in prs]
    uv = [jnp.concatenate([u[i], vv[i]], axis=0) for i in prs]
    y = [us[i][n:] + _dgb(jnp.concatenate([m_rb[i], m_rk[i]], axis=1),
                             jnp.concatenate([_bd(u[i], m0), _bd(vv[i], m0)], axis=0)) for i in prs]
    for i in prs:
        sl = sls[i]
        bk = jnp.concatenate([bv[i] * e_l[:, sl], k2[:, sl] * e_l[:, sl]], axis=0)
        z = _dgb(uv[i], bk, TN)
        s_ref[0, i] = s0[i] * e_c[n - 1:n, sl] + jnp.where(ms, z[:RW_HEAD], z[RW_HEAD:])
    for i in prs:
        sl = sls[i]
        mean = _headsum_pair(y[i], m0) * (1.0 / RW_HEAD)
        d = y[i] - mean
        var = _headsum_pair(d * d, m0) * (1.0 / RW_HEAD)
        yn = d * lax.rsqrt(var + RW_GN_EPS) * lnw[:, sl] + lnb[:, sl]
        bonus = _headsum_pair(r[:, sl] * k2[:, sl] * rk[:, sl], m0) * vv[i]
        y_ref[0, :, sl] = ((yn + bonus) * g[:, sl]).astype(y_ref.dtype)


def _rwkv_weights(w2, a2, g2):
    lw = w2.shape[0]
    w2p = jnp.concatenate([w2, jnp.zeros((LANES - lw, BRANCH_W), F32)], axis=0).astype(BF16)
    a2p = jnp.concatenate([jnp.zeros((lw, BRANCH_W), F32), a2], axis=0).astype(BF16)
    g2p = jnp.pad(g2, ((0, RW_PAD - 3 * BRANCH_W - LANES - g2.shape[0]), (0, 0))).astype(BF16)
    return w2p, a2p, g2p


def _row(x):
    return x.reshape(1, -1)


def _const_spec(shape):
    nd = len(shape)
    return pl.BlockSpec(shape, lambda *_: (0,) * nd)


def _chained_call(kernel_fn, prev, inputs, in_specs, **kw):
    if prev is None:
        return pl.pallas_call(kernel_fn, in_specs=in_specs, **kw)(*inputs)
    n_in = len(inputs)

    def body(*refs):
        return kernel_fn(*refs[:n_in], *refs[n_in + 1:])

    return pl.pallas_call(body, in_specs=list(in_specs) + [pl.BlockSpec(memory_space=pl.ANY)],
                          input_output_aliases={n_in: 0}, **kw)(*inputs, prev)


def _put_kernel(rows_ref, o_ref):
    o_ref[...] = rows_ref[...].astype(o_ref.dtype)


def _put_rows(buf, rows, blk):
    g, n, c = rows.shape
    return _chained_call(
        _put_kernel, buf, (rows,), [_const_spec(rows.shape)],
        grid=(1,),
        out_specs=pl.BlockSpec((g, n, c), lambda i: (0, blk, 0)),
        out_shape=jax.ShapeDtypeStruct(buf.shape, buf.dtype),
        compiler_params=_cparams(("arbitrary",)),
        name="put_rows",
    )


def _rwkv_prompt(proj, nb, t, wts, y3, rows):
    nc = t // CHUNK
    consts = wts
    in_specs = [pl.BlockSpec((CHUNK, RW_PAD), lambda b, c: (b * nc + c, 0))]
    in_specs += [_const_spec(x.shape) for x in consts]
    return _chained_call(
        _rwkv_p_kernel, y3, (proj, *consts), in_specs,
        grid=(nb, nc),
        out_specs=[pl.BlockSpec((1, CHUNK, BRANCH_W), lambda b, c: (0, b * nc + c, 0)),
                   pl.BlockSpec((1, RW_HEADS // 2, RW_HEAD, LANES), lambda b, c: (b, 0, 0, 0))],
        out_shape=[jax.ShapeDtypeStruct((N_BRANCH, rows, BRANCH_W), BF16),
                   jax.ShapeDtypeStruct((nb, RW_HEADS // 2, RW_HEAD, LANES), F32)],
        scratch_shapes=[pltpu.VMEM((8, RW_PAD), F32)],
        compiler_params=_cparams(("parallel", "arbitrary")),
        name="rwkv_prompt",
    )


def _unpack_rwkv_state(sp):
    nb = sp.shape[0]
    s = sp.reshape(nb, RW_HEADS // 2, RW_HEAD, 2, RW_HEAD)
    return jnp.transpose(s, (0, 1, 3, 2, 4)).reshape(nb, RW_HEADS, RW_HEAD, RW_HEAD)


def _rwkv_s_prep_kernel(p_ref, prev_ref, mu_ref, w0_ref, w2_ref, a0_ref, a2_ref, g2_ref, kk_ref, ka_ref, rk_ref,
                        o_ref):
    r, k2, v, lw, a, g, kkraw = _rwkv_tokens(p_ref[...], prev_ref[...], mu_ref[...], w0_ref[...], w2_ref[...],
                                             a0_ref[...], a2_ref[...], g2_ref[...], kk_ref[...], ka_ref[...])
    n = r.shape[0]
    m0, _, _ = _pair_masks(n)
    rk = rk_ref[...]
    o_ref[0] = r
    o_ref[1] = jnp.exp(lw)
    o_ref[2] = k2
    o_ref[3] = v
    o_ref[6] = g
    for pi in range(RW_HEADS // 2):
        sl = slice(pi * LANES, (pi + 1) * LANES)
        kkp = kkraw[:, sl]
        kkn = kkp / jnp.maximum(jnp.sqrt(_headsum_pair(kkp * kkp, m0)), 1e-12)
        o_ref[4, :, sl] = -kkn
        o_ref[5, :, sl] = kkn * a[:, sl]
        o_ref[7, :, sl] = _headsum_pair(r[:, sl] * k2[:, sl] * rk[:, sl], m0) * v[:, sl]


def _rwkv_s_kernel(vec_ref, s_ref, lnw_ref, lnb_ref, so_ref, y_ref):
    nbb = s_ref.shape[1]
    n = RW_HEAD
    eye = lax.broadcasted_iota(jnp.int32, (n, n), 0) == lax.broadcasted_iota(jnp.int32, (n, n), 1)

    def body(bi, carry):
        for h in range(RW_HEADS):
            hs = pl.ds(h, 1)
            s = s_ref[0, bi, h]
            r_ = vec_ref[0, bi, hs, :]
            w_ = vec_ref[1, bi, hs, :]
            k_ = vec_ref[2, bi, hs, :]
            v_ = vec_ref[3, bi, hs, :]
            a_ = vec_ref[4, bi, hs, :]
            b_ = vec_ref[5, bi, hs, :]
            g_ = vec_ref[6, bi, hs, :]
            bonus_ = vec_ref[7, bi, hs, :]
            sa = jnp.sum(s * a_, axis=-1, keepdims=True)
            vc = jnp.sum(jnp.where(eye, v_, 0.0), axis=-1, keepdims=True)
            sn = s * w_ + sa * b_ + vc * k_
            so_ref[0, bi, h] = sn
            yc = jnp.sum(sn * r_, axis=-1, keepdims=True)
            yr = jnp.sum(jnp.where(eye, yc, 0.0), axis=0, keepdims=True)
            mean = jnp.mean(yr, axis=-1, keepdims=True)
            d = yr - mean
            var = jnp.mean(d * d, axis=-1, keepdims=True)
            yn = d * lax.rsqrt(var + RW_GN_EPS) * lnw_ref[hs, :] + lnb_ref[hs, :]
            y_ref[bi, hs, :] = (yn + bonus_) * g_
        return carry

    lax.fori_loop(0, nbb, body, 0)


def _rwkv_sample(proj, row0_blk, shift_prev, state, layer, wts, lnw, lnb, s_all):
    nb = shift_prev.shape[0]
    consts = wts
    vec = pl.pallas_call(
        _rwkv_s_prep_kernel,
        grid=(1,),
        in_specs=[pl.BlockSpec((nb, RW_PAD), lambda i: (row0_blk, 0)),
                  pl.BlockSpec((nb, RW_PAD), lambda i: (0, 0))] + [_const_spec(x.shape) for x in consts],
        out_specs=pl.BlockSpec((8, nb, BRANCH_W), lambda i: (0, 0, 0)),
        out_shape=jax.ShapeDtypeStruct((8, nb, BRANCH_W), F32),
        compiler_params=_cparams(("arbitrary",)),
        name="rwkv_sample_prep",
    )(proj, shift_prev, *consts)
    vec = vec.reshape(8, nb, RW_HEADS, RW_HEAD)
    bb = 8
    st_spec = pl.BlockSpec((1, bb, RW_HEADS, RW_HEAD, RW_HEAD), lambda i: (layer, i, 0, 0, 0))
    s_all, y = _chained_call(
        _rwkv_s_kernel, s_all,
        (vec, state, lnw.reshape(RW_HEADS, RW_HEAD), lnb.reshape(RW_HEADS, RW_HEAD)),
        [pl.BlockSpec((8, bb, RW_HEADS, RW_HEAD), lambda i: (0, i, 0, 0)), st_spec,
         _const_spec((RW_HEADS, RW_HEAD)), _const_spec((RW_HEADS, RW_HEAD))],
        grid=(nb // bb,),
        out_specs=[st_spec, pl.BlockSpec((bb, RW_HEADS, RW_HEAD), lambda i: (i, 0, 0))],
        out_shape=[jax.ShapeDtypeStruct(state.shape, F32),
                   jax.ShapeDtypeStruct((nb, RW_HEADS, RW_HEAD), F32)],
        compiler_params=_cparams(("parallel",)),
        name="rwkv_sample",
    )
    return y.reshape(nb, BRANCH_W), s_all


def _expand_heads():
    k = lax.broadcasted_iota(jnp.int32, (LANES, BRANCH_W), 0)
    c = lax.broadcasted_iota(jnp.int32, (LANES, BRANCH_W), 1)
    return jnp.where(jnp.right_shift(c, 6) == k, 1.0, 0.0).astype(BF16)


def _ssd_p_kernel(p_ref, cw_ref, cb_ref, dtb_ref, alog_ref, dsk_ref, nw_ref, y_ref, st_ref, buf_scr):
    cidx = pl.program_id(1)
    n = p_ref.shape[0]

    @pl.when(cidx == 0)
    def _():
        st_ref[...] = jnp.zeros_like(st_ref)
        buf_scr[0:8, :] = jnp.zeros((8, SSD_CONV_DIM), F32)

    z = p_ref[:, 0:BRANCH_W]
    buf_scr[8:8 + n, :] = p_ref[:, BRANCH_W:BRANCH_W + SSD_CONV_DIM]
    conv = cb_ref[...]
    for i in range(SSD_CONV):
        conv = conv + cw_ref[i:i + 1, :] * buf_scr[pl.ds(8 - (SSD_CONV - 1) + i, n), :]
    buf_scr[0:8, :] = buf_scr[n:n + 8, :]
    xa = _silu(conv)
    xs = xa[:, 0:BRANCH_W]
    bm = xa[:, BRANCH_W:BRANCH_W + SSD_GROUPS * SSD_STATE]
    cm = xa[:, BRANCH_W + SSD_GROUPS * SSD_STATE:]
    dt = _softplus(p_ref[:, BRANCH_W + SSD_CONV_DIM:SSD_PAD] + dtb_ref[...])
    dte = _dot_sel_r(dt, _expand_heads())
    da = dte * (-jnp.exp(alog_ref[...]))
    cum = _dot_sel_l(_tri_incl(n), da)
    xdt = xs * dte
    ecum = jnp.exp(cum)
    cl = cum[n - 1:n, :]
    xdl = xdt * jnp.exp(cl - cum)
    pl_ = jnp.exp(cl)

    m0, row, col = _pair_masks(n)
    incl = col <= row
    eye = col == row
    gw = BRANCH_W // SSD_GROUPS
    ppg = gw // LANES
    ys = []
    for gi in range(SSD_GROUPS):
        gs = slice(gi * gw, (gi + 1) * gw)
        bg = bm[:, gi * SSD_STATE:(gi + 1) * SSD_STATE]
        cg = cm[:, gi * SSD_STATE:(gi + 1) * SSD_STATE]
        cbp = _dgb(cg, jnp.concatenate([bg, bg], axis=0), NT)
        st = st_ref[0, :, gs]
        cs = _dgb(cg, st)
        for q in range(ppg):
            sl = slice(gi * gw + q * LANES, gi * gw + (q + 1) * LANES)
            cp = cum[:, sl]
            rp = jnp.sum(jnp.where(eye, cp, 0.0), axis=0, keepdims=True)
            seg = jnp.exp(jnp.where(incl, cp - rp, -jnp.inf))
            yp = _dgb(cbp * seg, _bd(xdt[:, sl], m0)) + ecum[:, sl] * cs[:, q * LANES:(q + 1) * LANES]
            ys.append(yp)
        st_ref[0, :, gs] = st * pl_[:, gs] + _dgb(bg, xdl[:, gs], TN)
    y = jnp.concatenate(ys, axis=1) + xs * dsk_ref[...]
    y = y * _silu(z)
    for gi in range(SSD_GROUPS):
        gs = slice(gi * gw, (gi + 1) * gw)
        yg = y[:, gs]
        yg = yg * lax.rsqrt(jnp.mean(yg * yg, axis=-1, keepdims=True) + GROUP_EPS)
        y_ref[0, :, gs] = (yg * nw_ref[:, gs]).astype(y_ref.dtype)


def _ssd_consts(conv_w, conv_b, dt_bias, a_log, d_skip, norm_w):
    dtb = jnp.pad(dt_bias, (0, LANES - SSD_HEADS)).reshape(1, LANES)
    return (conv_w, _row(conv_b), dtb, _row(jnp.repeat(a_log, SSD_HEAD)), _row(jnp.repeat(d_skip, SSD_HEAD)),
            _row(norm_w))


def _ssd_prompt(proj, nb, t, consts, y3):
    nc = t // CHUNK
    return _chained_call(
        _ssd_p_kernel, y3, (proj, *consts),
        [pl.BlockSpec((CHUNK, SSD_PAD), lambda b, c: (b * nc + c, 0))] + [_const_spec(x.shape) for x in consts],
        grid=(nb, nc),
        out_specs=[pl.BlockSpec((1, CHUNK, BRANCH_W), lambda b, c: (1, b * nc + c, 0)),
                   pl.BlockSpec((1, SSD_STATE, BRANCH_W), lambda b, c: (b, 0, 0))],
        out_shape=[jax.ShapeDtypeStruct(y3.shape, BF16),
                   jax.ShapeDtypeStruct((nb, SSD_STATE, BRANCH_W), F32)],
        scratch_shapes=[pltpu.VMEM((CHUNK + 8, SSD_CONV_DIM), F32)],
        compiler_params=_cparams(("parallel", "arbitrary")),
        name="ssd_prompt",
    )


def _unpack_ssd_state(st):
    nb = st.shape[0]
    return jnp.transpose(st.reshape(nb, SSD_STATE, SSD_HEADS, SSD_HEAD), (0, 2, 3, 1))


def _ssd_s_prep_kernel(p_ref, cv_ref, cw_ref, cb_ref, dtb_ref, alog_ref, dsk_ref, o_ref, bc_ref):
    z = p_ref[:, 0:BRANCH_W]
    conv = cb_ref[...] + cw_ref[SSD_CONV - 1:SSD_CONV, :] * p_ref[:, BRANCH_W:BRANCH_W + SSD_CONV_DIM]
    for i in range(SSD_CONV - 1):
        conv = conv + cw_ref[i:i + 1, :] * cv_ref[i]
    xa = _silu(conv)
    xs = xa[:, 0:BRANCH_W]
    dt = _softplus(p_ref[:, BRANCH_W + SSD_CONV_DIM:SSD_PAD] + dtb_ref[...])
    dte = _dot_sel_r(dt, _expand_heads())
    o_ref[0] = xs * dte
    o_ref[1] = jnp.exp(dte * (-jnp.exp(alog_ref[...])))
    o_ref[2] = xs * dsk_ref[...]
    o_ref[3] = _silu(z)
    bc_ref[...] = xa[:, BRANCH_W:]


def _ssd_s_kernel(vec_ref, bc_ref, s_ref, nw_ref, so_ref, y_ref):
    nbb = s_ref.shape[1]
    n = SSD_HEAD
    eye = lax.broadcasted_iota(jnp.int32, (n, n), 0) == lax.broadcasted_iota(jnp.int32, (n, n), 1)
    hpg = SSD_HEADS // SSD_GROUPS

    def body(bi, carry):
        for h in range(SSD_HEADS):
            hs = pl.ds(h, 1)
            gi = h // hpg
            s = s_ref[0, bi, h]
            xdt_ = vec_ref[0, bi, hs, :]
            dec = vec_ref[1, bi, hs, 0:1]
            b_ = bc_ref[bi, pl.ds(gi, 1), :]
            c_ = bc_ref[bi, pl.ds(SSD_GROUPS + gi, 1), :]
            xc = jnp.sum(jnp.where(eye, xdt_, 0.0), axis=-1, keepdims=True)
            sn = s * dec + xc * b_
            so_ref[0, bi, h] = sn
            yc = jnp.sum(sn * c_, axis=-1, keepdims=True)
            yr = jnp.sum(jnp.where(eye, yc, 0.0), axis=0, keepdims=True)
            y_ref[bi, hs, :] = (yr + vec_ref[2, bi, hs, :]) * vec_ref[3, bi, hs, :]
        for gi in range(SSD_GROUPS):
            rs = pl.ds(gi * hpg, hpg)
            yg = y_ref[bi, rs, :]
            ms = jnp.sum(jnp.sum(yg * yg, axis=-1, keepdims=True), axis=0, keepdims=True) * (1.0 / (hpg * n))
            y_ref[bi, rs, :] = yg * lax.rsqrt(ms + GROUP_EPS) * nw_ref[rs, :]
        return carry

    lax.fori_loop(0, nbb, body, 0)


def _ssd_sample(proj, row0_blk, conv_prev, state, layer, consts, s_all):
    nb = conv_prev.shape[1]
    cw, cb, dtb, alog, dsk, nw = consts
    vec, bc = pl.pallas_call(
        _ssd_s_prep_kernel,
        grid=(1,),
        in_specs=[pl.BlockSpec((nb, SSD_PAD), lambda i: (row0_blk, 0)),
                  _const_spec(conv_prev.shape)] + [_const_spec(x.shape) for x in (cw, cb, dtb, alog, dsk)],
        out_specs=[_const_spec((4, nb, BRANCH_W)), _const_spec((nb, 2 * SSD_GROUPS * SSD_STATE))],
        out_shape=[jax.ShapeDtypeStruct((4, nb, BRANCH_W), F32),
                   jax.ShapeDtypeStruct((nb, 2 * SSD_GROUPS * SSD_STATE), F32)],
        compiler_params=_cparams(("arbitrary",)),
        name="ssd_sample_prep",
    )(proj, conv_prev, cw, cb, dtb, alog, dsk)
    vec = vec.reshape(4, nb, SSD_HEADS, SSD_HEAD)
    bc = bc.reshape(nb, 2 * SSD_GROUPS, SSD_STATE)
    bb = 8
    st_spec = pl.BlockSpec((1, bb, SSD_HEADS, SSD_HEAD, SSD_STATE), lambda i: (layer, i, 0, 0, 0))
    s_all, y = _chained_call(
        _ssd_s_kernel, s_all, (vec, bc, state, nw.reshape(SSD_HEADS, SSD_HEAD)),
        [pl.BlockSpec((4, bb, SSD_HEADS, SSD_HEAD), lambda i: (0, i, 0, 0)),
         pl.BlockSpec((bb, 2 * SSD_GROUPS, SSD_STATE), lambda i: (i, 0, 0)), st_spec,
         _const_spec((SSD_HEADS, SSD_HEAD))],
        grid=(nb // bb,),
        out_specs=[st_spec, pl.BlockSpec((bb, SSD_HEADS, SSD_HEAD), lambda i: (i, 0, 0))],
        out_shape=[jax.ShapeDtypeStruct(state.shape, F32),
                   jax.ShapeDtypeStruct((nb, SSD_HEADS, SSD_HEAD), F32)],
        compiler_params=_cparams(("parallel",)),
        name="ssd_sample",
    )
    return y.reshape(nb, BRANCH_W), s_all


def _gla_tokens(p, au, ab):
    q = p[:, 0:GLA_DK] * (GLA_HK ** -0.5)
    k = p[:, GLA_DK:2 * GLA_DK]
    v = p[:, 2 * GLA_DK:2 * GLA_DK + GLA_DV]
    r = p[:, 2 * GLA_DK + GLA_DV:2 * GLA_DK + 2 * GLA_DV]
    ad = p[:, 2 * GLA_DK + 2 * GLA_DV:GLA_PAD]
    lg = -_softplus(-(_dgb(ad, au) + ab)) * (1.0 / GLA_TAU)
    return q, k, v, r, lg


def _gla_p_kernel(p_ref, au_ref, ab_ref, nw_ref, o_ref, st_ref):
    cidx = pl.program_id(1)

    @pl.when(cidx == 0)
    def _():
        st_ref[...] = jnp.zeros_like(st_ref)

    n = p_ref.shape[0]
    q, k, v, r, lg = _gla_tokens(p_ref[...], au_ref[...], ab_ref[...])
    cum = _dot_sel_l(_tri_incl(n), lg)
    cumx = cum - lg
    cl = cum[n - 1:n, :]
    qe = q * jnp.exp(cum)
    kb = k * jnp.exp(cl - cum)
    pl_ = jnp.exp(cl)
    rowi = lax.broadcasted_iota(jnp.int32, (SUB, 1), 0)
    for h in range(GLA_HEADS):
        sk = slice(h * GLA_HK, (h + 1) * GLA_HK)
        sv = slice(h * GLA_HV, (h + 1) * GLA_HV)
        st = st_ref[0, h]
        vh = v[:, sv]
        parts = []
        for blk in range(n // SUB):
            lo = blk * SUB
            rs = slice(lo, lo + SUB)
            q_i = q[rs, sk]
            c_i = cum[rs, sk]
            k_i = k[rs, sk]
            v_i = vh[rs]
            if blk > 0:
                cref = cumx[lo:lo + 1, sk]
                qt = q_i * jnp.exp(c_i - cref)
                kt = k[0:lo, sk] * jnp.exp(cref - cum[0:lo, sk])
                acc = _dgb(_dgb(qt, kt, NT), vh[0:lo])
            else:
                acc = jnp.zeros((SUB, GLA_HV), F32)
            for s in range(SUB):
                e = jnp.exp(c_i - c_i[s:s + 1])
                w = jnp.sum(q_i * k_i[s:s + 1] * e, axis=-1, keepdims=True)
                acc = acc + jnp.where(rowi >= s, w, 0.0) * v_i[s:s + 1]
            parts.append(acc)
        o = jnp.concatenate(parts, axis=0) + _dgb(qe[:, sk], st, NT)
        st_ref[0, h] = st * pl_[:, sk] + _dgb(vh, kb[:, sk], TN)
        o = o * lax.rsqrt(jnp.mean(o * o, axis=-1, keepdims=True) + GROUP_EPS) * nw_ref[...]
        o_ref[0, :, sv] = (o * _silu(r[:, sv])).astype(o_ref.dtype)


def _gla_consts(alpha_up, alpha_b, norm_w):
    au = jnp.pad(alpha_up, ((0, LANES - GLA_LORA), (0, 0))).astype(BF16)
    return au, _row(alpha_b), _row(norm_w)


def _gla_prompt(proj, nb, t, consts, y3):
    nc = t // CHUNK
    return _chained_call(
        _gla_p_kernel, y3, (proj, *consts),
        [pl.BlockSpec((CHUNK, GLA_PAD), lambda b, c: (b * nc + c, 0))] + [_const_spec(x.shape) for x in consts],
        grid=(nb, nc),
        out_specs=[pl.BlockSpec((1, CHUNK, GLA_DV), lambda b, c: (2, b * nc + c, 0)),
                   pl.BlockSpec((1, GLA_HEADS, GLA_HV, GLA_HK), lambda b, c: (b, 0, 0, 0))],
        out_shape=[jax.ShapeDtypeStruct(y3.shape, BF16),
                   jax.ShapeDtypeStruct((nb, GLA_HEADS, GLA_HV, GLA_HK), F32)],
        compiler_params=_cparams(("parallel", "arbitrary")),
        name="gla_prompt",
    )


def _gla_s_prep_kernel(p_ref, au_ref, ab_ref, qk_ref, vr_ref):
    q, k, v, r, lg = _gla_tokens(p_ref[...], au_ref[...], ab_ref[...])
    qk_ref[0] = q
    qk_ref[1] = k
    qk_ref[2] = jnp.exp(lg)
    vr_ref[0] = v
    vr_ref[1] = _silu(r)


def _gla_s_kernel(qk_ref, vr_ref, s_ref, nw_ref, so_ref, y_ref):
    nbb = s_ref.shape[1]
    n = GLA_HK
    eye = lax.broadcasted_iota(jnp.int32, (n, n), 0) == lax.broadcasted_iota(jnp.int32, (n, n), 1)

    def col(x):
        return jnp.sum(jnp.where(eye, x, 0.0), axis=-1, keepdims=True)

    def body(bi, carry):
        for h in range(GLA_HEADS):
            hs = pl.ds(h, 1)
            s = s_ref[0, bi, h]
            sn = s * col(qk_ref[2, bi, hs, :]) + col(qk_ref[1, bi, hs, :]) * vr_ref[0, bi, hs, :]
            so_ref[0, bi, h] = sn
            o = jnp.sum(sn * col(qk_ref[0, bi, hs, :]), axis=0, keepdims=True)
            o = o * lax.rsqrt(jnp.mean(o * o, axis=-1, keepdims=True) + GROUP_EPS) * nw_ref[...]
            y_ref[bi, hs, :] = o * vr_ref[1, bi, hs, :]
        return carry

    lax.fori_loop(0, nbb, body, 0)


def _gla_sample(proj, row0_blk, nb, state, layer, consts, s_all):
    au, ab, nw = consts
    qk, vr = pl.pallas_call(
        _gla_s_prep_kernel,
        grid=(1,),
        in_specs=[pl.BlockSpec((nb, GLA_PAD), lambda i: (row0_blk, 0)), _const_spec(au.shape),
                  _const_spec(ab.shape)],
        out_specs=[_const_spec((3, nb, GLA_DK)), _const_spec((2, nb, GLA_DV))],
        out_shape=[jax.ShapeDtypeStruct((3, nb, GLA_DK), F32), jax.ShapeDtypeStruct((2, nb, GLA_DV), F32)],
        compiler_params=_cparams(("arbitrary",)),
        name="gla_sample_prep",
    )(proj, au, ab)
    qk = qk.reshape(3, nb, GLA_HEADS, GLA_HK)
    vr = vr.reshape(2, nb, GLA_HEADS, GLA_HV)
    bb = 8
    st_spec = pl.BlockSpec((1, bb, GLA_HEADS, GLA_HK, GLA_HV), lambda i: (layer, i, 0, 0, 0))
    s_all, y = _chained_call(
        _gla_s_kernel, s_all, (qk, vr, state, nw),
        [pl.BlockSpec((3, bb, GLA_HEADS, GLA_HK), lambda i: (0, i, 0, 0)),
         pl.BlockSpec((2, bb, GLA_HEADS, GLA_HV), lambda i: (0, i, 0, 0)), st_spec, _const_spec(nw.shape)],
        grid=(nb // bb,),
        out_specs=[st_spec, pl.BlockSpec((bb, GLA_HEADS, GLA_HV), lambda i: (i, 0, 0))],
        out_shape=[jax.ShapeDtypeStruct(state.shape, F32),
                   jax.ShapeDtypeStruct((nb, GLA_HEADS, GLA_HV), F32)],
        compiler_params=_cparams(("parallel",)),
        name="gla_sample",
    )
    return y.reshape(nb, GLA_DV), s_all


def kernel(x_prompt, x_sample, state_rwkv, state_rwkv_shift, state_ssd, state_ssd_conv, state_gla, cache_mem_k, cache_mem_v, mem_prompt, norm_mix_pre, norm_mix_post, norm_x_pre, norm_x_post, norm_ffn_pre, norm_ffn_post, w_in, rw_mu, rw_w0, rw_w2, rw_a0, rw_a2, rw_g2, rw_kk, rw_ka, rw_rk, rw_ln_w, rw_ln_b, ssd_conv_w, ssd_conv_b, ssd_dt_bias, ssd_a_log, ssd_d, ssd_norm_w, gla_alpha_up, gla_alpha_b, gla_norm_w, w_branch, w_out, x_mem_norm, x_wq, x_wk, x_wv, x_wo, ffn_up, ffn_down):
    nbp, t, d = x_prompt.shape
    nbs = x_sample.shape[0]
    depth = w_in.shape[0]
    mp = nbp * t
    m = mp + nbs
    sblk = mp // nbs

    x = jnp.concatenate([x_prompt.reshape(mp, d), x_sample.reshape(nbs, d)], axis=0)
    mem_rows = mem_prompt.reshape(nbp * N_MEM, d)
    tm_norm = _row_tile(m, (1040, 1024, 512, 256, 128, 64, 8))

    w_rw_all, w_ssd_all, w_gla_all, w_gate_all = _split_w_in(w_in)
    wb_all = _cast_bf16(w_branch.reshape(depth * N_BRANCH, BRANCH_W, d)).reshape(depth, N_BRANCH, BRANCH_W, d)
    w_out_b, wq_b, wk_b, wv_b, wo_b, up_b, down_b = [_cast_bf16(w) for w in
                                                     (w_out, x_wq, x_wk, x_wv, x_wo, ffn_up, ffn_down)]

    xn = _norm_rows(x, norm_mix_pre[0], tm_norm)
    p_acc = [[] for _ in range(7)]
    s_small = [[] for _ in range(2)]
    rw_ss = ssd_ss = gla_ss = None
    for l in range(depth):
        proj_rw = _mm(xn, w_rw_all, l, name="proj_rwkv")
        proj_ssd = _mm(xn, w_ssd_all, l, name="proj_ssd")
        proj_gla = _mm(xn, w_gla_all, l, name="proj_gla")
        gate = _mm(xn, w_gate_all, l, name="proj_gate")

        w2p, a2p, g2p = _rwkv_weights(rw_w2[l], rw_a2[l], rw_g2[l])
        rw_tok = (_row(jnp.pad(rw_mu[l], (0, RW_PAD - RW_PROJ))), _row(rw_w0[l]), w2p, _row(rw_a0[l]), a2p, g2p,
                  _row(rw_kk[l]), _row(rw_ka[l]), _row(rw_rk[l]))
        y3, rw_sp = _rwkv_prompt(proj_rw, nbp, t, rw_tok + (_row(rw_ln_w[l]), _row(rw_ln_b[l])), None, m)
        shift_prev = jnp.pad(state_rwkv_shift[l], ((0, 0), (0, RW_PAD - RW_PROJ)))
        ya_s, rw_ss = _rwkv_sample(proj_rw, sblk, shift_prev, state_rwkv, l, rw_tok, rw_ln_w[l], rw_ln_b[l], rw_ss)

        ssd_c = _ssd_consts(ssd_conv_w[l], ssd_conv_b[l], ssd_dt_bias[l], ssd_a_log[l], ssd_d[l], ssd_norm_w[l])
        y3, ssd_sp = _ssd_prompt(proj_ssd, nbp, t, ssd_c, y3)
        conv_prev = jnp.transpose(state_ssd_conv[l], (1, 0, 2))
        yb_s, ssd_ss = _ssd_sample(proj_ssd, sblk, conv_prev, state_ssd, l, ssd_c, ssd_ss)

        gla_c = _gla_consts(gla_alpha_up[l], gla_alpha_b[l], gla_norm_w[l])
        y3, gla_sp = _gla_prompt(proj_gla, nbp, t, gla_c, y3)
        yc_s, gla_ss = _gla_sample(proj_gla, sblk, nbs, state_gla, l, gla_c, gla_ss)

        y3 = _put_rows(y3, jnp.stack([ya_s, yb_s, yc_s]), sblk)
        mix = _merge(y3, wb_all, l, gate)
        x, xn = _mm_resnorm(mix, w_out_b, l, x, norm_mix_post[l], norm_x_pre[l], name="out_proj")

        mn = _norm_rows(mem_rows, x_mem_norm[l], _row_tile(mem_rows.shape[0], (1024, 512, 256, 128, 64, 8)))
        mk = _mm(mn, wk_b, l, name="mem_k")
        mv = _mm(mn, wv_b, l, name="mem_v")
        q = _mm(xn, wq_b, l, out_dtype=BF16, name="attn_q")
        o = _attn_prompt(q, mk.reshape(nbp, N_MEM, d), mv.reshape(nbp, N_MEM, d), nbp, t)
        o_s = _attn_sample(q[mp:].reshape(nbs, X_HEADS, X_HEAD), cache_mem_k, cache_mem_v, l)
        o = _put_rows(o.reshape(1, m, d), o_s.reshape(1, nbs, d), sblk).reshape(m, d)
        x, xn = _mm_resnorm(o, wo_b, l, x, norm_x_post[l], norm_ffn_pre[l], name="attn_out")

        hf = _mm(xn, up_b, l, out_dtype=BF16, act="relu2", name="ffn_up")
        g_next = norm_mix_pre[l + 1] if l + 1 < depth else norm_mix_pre[l]
        x, xn = _mm_resnorm(hf, down_b, l, x, norm_ffn_post[l], g_next, name="ffn_down")

        last = [b * t + t - 1 for b in range(nbp)]
        rw_shift_p = jnp.stack([proj_rw[i, :RW_PROJ] for i in last])
        ssd_conv_p = jnp.stack([proj_ssd[i - (SSD_CONV - 2):i + 1, BRANCH_W:BRANCH_W + SSD_CONV_DIM] for i in last])
        new_p = (_unpack_rwkv_state(rw_sp), rw_shift_p, _unpack_ssd_state(ssd_sp), ssd_conv_p,
                 jnp.transpose(gla_sp, (0, 1, 3, 2)),
                 mk.reshape(nbp, N_MEM, X_HEADS, X_HEAD), mv.reshape(nbp, N_MEM, X_HEADS, X_HEAD))
        xbc_s = proj_ssd[mp:, BRANCH_W:BRANCH_W + SSD_CONV_DIM]
        new_s = (proj_rw[mp:, :RW_PROJ],
                 jnp.concatenate([state_ssd_conv[l][:, 1:], xbc_s[:, None, :]], axis=1))
        for acc, val in zip(p_acc, new_p):
            acc.append(val)
        for acc, val in zip(s_small, new_s):
            acc.append(val)

    outs_p = [jnp.stack(a) for a in p_acc]
    s_shift, s_conv = [jnp.stack(a) for a in s_small]
    return (x[:mp].reshape(nbp, t, d), x[mp:].reshape(nbs, 1, d), *outs_p,
            rw_ss, s_shift, ssd_ss, s_conv, gla_ss)
```

```python
import functools

import jax
import jax.numpy as jnp
from jax import lax
from jax.experimental import pallas as pl
from jax.experimental.pallas import tpu as pltpu

F32 = jnp.float32
BF16 = jnp.bfloat16

D_MODEL = 2048
BRANCH_W = 1024
N_BRANCH = 3
RW_HEADS = 16
RW_HEAD = 64
RW_PROJ = 3360
RW_PAD = 3456
RW_GN_EPS = 64e-5
SSD_HEADS = 16
SSD_HEAD = 64
SSD_GROUPS = 2
SSD_STATE = 128
SSD_CONV = 4
SSD_CONV_DIM = 1536
SSD_PROJ = 2576
SSD_PAD = 2688
GLA_HEADS = 4
GLA_DK = 512
GLA_DV = 1024
GLA_HK = 128
GLA_HV = 256
GLA_LORA = 16
GLA_TAU = 16.0
GLA_PROJ = 3088
GLA_PAD = 3200
N_MEM = 256
X_HEADS = 4
X_HEAD = 512
D_FF = 8192
NORM_EPS = 1e-6
GROUP_EPS = 1e-5
CHUNK = 64
SUB = 16
LANES = 128
VMEM_LIMIT = 56 * 1024 * 1024

NN = ((1,), (0,))
NT = ((1,), (1,))
TN = ((0,), (0,))


def _cparams(sem):
    return pltpu.CompilerParams(dimension_semantics=sem, vmem_limit_bytes=VMEM_LIMIT)


def _dg(a, b, dims=NN):
    return lax.dot_general(a, b, (dims, ((), ())), preferred_element_type=F32)


def _dgb(a, b, dims=NN):
    return _dg(a.astype(BF16), b.astype(BF16), dims)


def _hl(x):
    h = x.astype(BF16)
    return h, (x - h.astype(F32)).astype(BF16)


def _dot_hi(a, b, dims=NN):
    ah, al = _hl(a)
    bh, bl = _hl(b)
    return _dg(ah, bh, dims) + (_dg(ah, bl, dims) + _dg(al, bh, dims))


def _split3(x):
    h = x.astype(BF16)
    r = x - h.astype(F32)
    m = r.astype(BF16)
    return h, m, (r - m.astype(F32)).astype(BF16)


def _dot_sel_l(sel, x):
    h, m, l = _split3(x)
    return _dg(sel, h) + (_dg(sel, m) + _dg(sel, l))


def _dot_sel_r(x, sel):
    h, m, l = _split3(x)
    return _dg(h, sel) + (_dg(m, sel) + _dg(l, sel))


def _softplus(x):
    return jnp.maximum(x, 0.0) + jnp.log1p(jnp.exp(-jnp.abs(x)))


def _sigmoid(x):
    return 1.0 / (1.0 + jnp.exp(-x))


def _silu(x):
    return x * _sigmoid(x)


def _tri_incl(n):
    r = lax.broadcasted_iota(jnp.int32, (n, n), 0)
    c = lax.broadcasted_iota(jnp.int32, (n, n), 1)
    return jnp.where(c <= r, 1.0, 0.0).astype(BF16)


def _pair_masks(rows):
    lane = lax.broadcasted_iota(jnp.int32, (rows, LANES), 1)
    row = lax.broadcasted_iota(jnp.int32, (rows, LANES), 0)
    m0 = lane < 64
    col = jnp.bitwise_and(lane, 63)
    return m0, row, col


def _headsum_pair(x, m0):
    s0 = jnp.sum(jnp.where(m0, x, 0.0), axis=-1, keepdims=True)
    s1 = jnp.sum(jnp.where(m0, 0.0, x), axis=-1, keepdims=True)
    return jnp.where(m0, s0, s1)


def _bd(x, m0):
    return jnp.concatenate([jnp.where(m0, x, 0.0), jnp.where(m0, 0.0, x)], axis=0)


def _norm_kernel(x_ref, g_ref, o_ref):
    x = x_ref[...]
    y = x * lax.rsqrt(jnp.mean(x * x, axis=-1, keepdims=True) + NORM_EPS)
    o_ref[...] = (y * g_ref[...]).astype(o_ref.dtype)


def _norm_rows(x, g, tm):
    m, d = x.shape
    return pl.pallas_call(
        _norm_kernel,
        grid=(m // tm,),
        in_specs=[pl.BlockSpec((tm, d), lambda i: (i, 0)), pl.BlockSpec((1, d), lambda i: (0, 0))],
        out_specs=pl.BlockSpec((tm, d), lambda i: (i, 0)),
        out_shape=jax.ShapeDtypeStruct((m, d), BF16),
        compiler_params=_cparams(("parallel",)),
        name="rmsnorm",
    )(x, g.reshape(1, d))


def _mm_kernel(a_ref, w_ref, o_ref, *scratch, nk, act):
    part = _dg(a_ref[...], w_ref[0])

    def finish(acc):
        if act == "relu2":
            acc = jnp.square(jnp.maximum(acc, 0.0))
        elif act == "sigmoid":
            acc = _sigmoid(acc)
        o_ref[...] = acc.astype(o_ref.dtype)

    if nk == 1:
        finish(part)
        return
    acc_ref, = scratch
    k = pl.program_id(2)

    @pl.when(k == 0)
    def _():
        acc_ref[...] = part

    @pl.when(k > 0)
    def _():
        acc_ref[...] += part

    @pl.when(k == nk - 1)
    def _():
        finish(acc_ref[...])


def _mm_resnorm_kernel(a_ref, w_ref, res_ref, gp_ref, gn_ref, x_ref, xn_ref, *scratch, nk):
    part = _dg(a_ref[...], w_ref[0])

    def finish(acc):
        y = acc * lax.rsqrt(jnp.mean(acc * acc, axis=-1, keepdims=True) + NORM_EPS) * gp_ref[...]
        x = res_ref[...] + y
        x_ref[...] = x
        xn = x * lax.rsqrt(jnp.mean(x * x, axis=-1, keepdims=True) + NORM_EPS) * gn_ref[...]
        xn_ref[...] = xn.astype(xn_ref.dtype)

    if nk == 1:
        finish(part)
        return
    acc_ref, = scratch
    k = pl.program_id(1)

    @pl.when(k == 0)
    def _():
        acc_ref[...] = part

    @pl.when(k > 0)
    def _():
        acc_ref[...] += part

    @pl.when(k == nk - 1)
    def _():
        finish(acc_ref[...])


def _pick(n, cands):
    for c in cands:
        if n % c == 0:
            return c
    return n


def _row_tile(m, cands):
    return _pick(m, cands)


def _mm(a, w, layer, out_dtype=F32, act=None, name="mm"):
    m, kd = a.shape
    n = w.shape[2]
    tm = _row_tile(m, (1040, 1024, 512, 256, 128, 64, 8))
    tn = _pick(n, (1024, 1152, 896, 640, 512, 384, 256, 128))
    tk = _pick(kd, (2048, 1024, 512))
    nk = kd // tk
    scratch = [pltpu.VMEM((tm, tn), F32)] if nk > 1 else []
    return pl.pallas_call(
        functools.partial(_mm_kernel, nk=nk, act=act),
        grid=(m // tm, n // tn, nk),
        in_specs=[pl.BlockSpec((tm, tk), lambda i, j, k: (i, k)),
                  pl.BlockSpec((1, tk, tn), lambda i, j, k: (layer, k, j))],
        out_specs=pl.BlockSpec((tm, tn), lambda i, j, k: (i, j)),
        out_shape=jax.ShapeDtypeStruct((m, n), out_dtype),
        scratch_shapes=scratch,
        compiler_params=_cparams(("parallel", "parallel", "arbitrary")),
        name=name,
    )(a, w)


def _mm_resnorm(a, w, layer, res, g_post, g_next, name="mm_resnorm"):
    m, kd = a.shape
    n = w.shape[2]
    tm = _row_tile(m, (520, 512, 256, 128, 64, 8))
    tk = _pick(kd, (2048, 1024, 512))
    res_spec = pl.BlockSpec((tm, n), lambda i, k: (i, 0))
    nk = kd // tk
    scratch = [pltpu.VMEM((tm, n), F32)] if nk > 1 else []
    return pl.pallas_call(
        functools.partial(_mm_resnorm_kernel, nk=nk),
        grid=(m // tm, nk),
        in_specs=[pl.BlockSpec((tm, tk), lambda i, k: (i, k)),
                  pl.BlockSpec((1, tk, n), lambda i, k: (layer, k, 0)),
                  res_spec,
                  pl.BlockSpec((1, n), lambda i, k: (0, 0)),
                  pl.BlockSpec((1, n), lambda i, k: (0, 0))],
        out_specs=[pl.BlockSpec((tm, n), lambda i, k: (i, 0)),
                   pl.BlockSpec((tm, n), lambda i, k: (i, 0))],
        out_shape=[jax.ShapeDtypeStruct((m, n), F32), jax.ShapeDtypeStruct((m, n), BF16)],
        scratch_shapes=scratch,
        compiler_params=_cparams(("parallel", "arbitrary")),
        name=name,
    )(a, w, res, g_post.reshape(1, n), g_next.reshape(1, n))


def _merge_kernel(y_ref, w_ref, g_ref, o_ref, acc_ref):
    n = pl.program_id(2)
    z = _dg(y_ref[0], w_ref[0, 0]) * g_ref[...].astype(F32)

    @pl.when(n == 0)
    def _():
        acc_ref[...] = z

    @pl.when(n > 0)
    def _():
        acc_ref[...] += z

    @pl.when(n == N_BRANCH - 1)
    def _():
        o_ref[...] = acc_ref[...].astype(o_ref.dtype)


def _merge(y3, wb, layer, gate):
    _, m, bw = y3.shape
    d = wb.shape[3]
    tm = _row_tile(m, (1040, 1024, 512, 256, 128, 64, 8))
    tn = 1024
    nj = d // tn
    return pl.pallas_call(
        _merge_kernel,
        grid=(m // tm, nj, N_BRANCH),
        in_specs=[pl.BlockSpec((1, tm, bw), lambda i, j, n: (n, i, 0)),
                  pl.BlockSpec((1, 1, bw, tn), lambda i, j, n: (layer, n, 0, j)),
                  pl.BlockSpec((tm, tn), lambda i, j, n: (i, n * nj + j))],
        out_specs=pl.BlockSpec((tm, tn), lambda i, j, n: (i, j)),
        out_shape=jax.ShapeDtypeStruct((m, d), BF16),
        scratch_shapes=[pltpu.VMEM((tm, tn), F32)],
        compiler_params=_cparams(("parallel", "parallel", "arbitrary")),
        name="merge",
    )(y3, wb, gate)


def _cast_kernel(w_ref, o_ref):
    o_ref[...] = w_ref[...].astype(o_ref.dtype)


def _cast_bf16(w):
    g, r, c = w.shape
    tr = _pick(r, (256, 128, 64, 8))
    return pl.pallas_call(
        _cast_kernel,
        grid=(g, r // tr),
        in_specs=[pl.BlockSpec((1, tr, c), lambda a, i: (a, i, 0))],
        out_specs=pl.BlockSpec((1, tr, c), lambda a, i: (a, i, 0)),
        out_shape=jax.ShapeDtypeStruct((g, r, c), BF16),
        compiler_params=_cparams(("parallel", "parallel")),
        name="cast_bf16",
    )(w)


_SEGS = ((0, RW_PROJ, RW_PAD),
         (RW_PROJ, SSD_PROJ, SSD_PAD),
         (RW_PROJ + SSD_PROJ, GLA_PROJ, GLA_PAD),
         (RW_PROJ + SSD_PROJ + GLA_PROJ, N_BRANCH * D_MODEL, N_BRANCH * D_MODEL))


def _seg_window(first):
    base = first // LANES * LANES
    return base, first - base


def _split_kernel(*refs):
    ins, outs = refs[:len(_SEGS)], refs[len(_SEGS):]
    for (first, live, padded), w_ref, o_ref in zip(_SEGS, ins, outs):
        _, off = _seg_window(first)
        width = -(-(off + live) // LANES) * LANES
        x = w_ref[0, :, 0:width]
        if off:
            x = pltpu.roll(x, width - off, 1)
        x = x[:, 0:padded]
        if live < padded:
            lane = lax.broadcasted_iota(jnp.int32, x.shape, 1)
            x = jnp.where(lane < live, x, 0.0)
        o_ref[0] = x.astype(o_ref.dtype)


def _split_w_in(w_in):
    depth, d, n_in = w_in.shape
    tr = 128
    in_specs = []
    for first, live, padded in _SEGS:
        base, off = _seg_window(first)
        bw = base if base else padded
        assert off + live <= bw and (base == 0 or base % bw == 0)
        in_specs.append(pl.BlockSpec((1, tr, bw), lambda l, i, b=(base // bw): (l, i, b)))
    return pl.pallas_call(
        _split_kernel,
        grid=(depth, d // tr),
        in_specs=in_specs,
        out_specs=[pl.BlockSpec((1, tr, p), lambda l, i: (l, i, 0)) for _, _, p in _SEGS],
        out_shape=[jax.ShapeDtypeStruct((depth, d, p), BF16) for _, _, p in _SEGS],
        compiler_params=_cparams(("parallel", "parallel")),
        name="split_w_in",
    )(*([w_in] * len(_SEGS)))


def _attn_p_kernel(q_ref, k_ref, v_ref, o_ref):
    s = _dgb(q_ref[...], k_ref[0], NT) * (X_HEAD ** -0.5)
    p = jnp.exp(s - jnp.max(s, axis=-1, keepdims=True))
    attn = p / jnp.sum(p, axis=-1, keepdims=True)
    o_ref[...] = _dgb(attn, v_ref[0]).astype(o_ref.dtype)


def _attn_prompt(q, mem_k, mem_v, nb, t):
    tq = _pick(t, (1024, 512, 256, 128, 64))
    nq = t // tq
    return pl.pallas_call(
        _attn_p_kernel,
        grid=(nb, X_HEADS, nq),
        in_specs=[pl.BlockSpec((tq, X_HEAD), lambda b, h, i: (b * nq + i, h)),
                  pl.BlockSpec((1, N_MEM, X_HEAD), lambda b, h, i: (b, 0, h)),
                  pl.BlockSpec((1, N_MEM, X_HEAD), lambda b, h, i: (b, 0, h))],
        out_specs=pl.BlockSpec((tq, X_HEAD), lambda b, h, i: (b * nq + i, h)),
        out_shape=jax.ShapeDtypeStruct(q.shape, BF16),
        compiler_params=_cparams(("parallel", "parallel", "parallel")),
        name="attn_prompt",
    )(q, mem_k, mem_v)


def _attn_s_kernel(q_ref, k_ref, v_ref, o_ref):
    q = q_ref[0].astype(F32)
    s = jnp.sum(k_ref[0, 0] * q[None], axis=-1, keepdims=True) * (X_HEAD ** -0.5)
    p = jnp.exp(s - jnp.max(s, axis=0, keepdims=True))
    attn = p / jnp.sum(p, axis=0, keepdims=True)
    o_ref[0] = jnp.sum(attn * v_ref[0, 0], axis=0).astype(o_ref.dtype)


def _attn_sample(q3, cache_k, cache_v, layer):
    nb = q3.shape[0]
    kv_spec = pl.BlockSpec((1, 1, N_MEM, X_HEADS, X_HEAD), lambda b: (layer, b, 0, 0, 0))
    return pl.pallas_call(
        _attn_s_kernel,
        grid=(nb,),
        in_specs=[pl.BlockSpec((1, X_HEADS, X_HEAD), lambda b: (b, 0, 0)), kv_spec, kv_spec],
        out_specs=pl.BlockSpec((1, X_HEADS, X_HEAD), lambda b: (b, 0, 0)),
        out_shape=jax.ShapeDtypeStruct((nb, X_HEADS, X_HEAD), BF16),
        compiler_params=_cparams(("parallel",)),
        name="attn_sample",
    )(q3, cache_k, cache_v)


def _rwkv_tokens(p, prev, mu, w0, w2, a0, a2, g2, k_k, k_a):
    c = BRANCH_W
    ps = p + (prev - p) * mu
    r = ps[:, 0:c]
    k = ps[:, c:2 * c]
    v = ps[:, 2 * c:3 * c]
    slab = ps[:, 3 * c:3 * c + 128]
    gslab = ps[:, 3 * c + 128:RW_PAD]
    wl = w0 + _dgb(jnp.tanh(slab), w2)
    lw = -jnp.exp(-_softplus(-wl) - 0.5)
    a = _sigmoid(a0 + _dgb(slab, a2))
    g = _dgb(_sigmoid(gslab), g2)
    kkraw = k * k_k
    k2 = k * (1.0 + (a - 1.0) * k_a)
    return r, k2, v, lw, a, g, kkraw


def _rwkv_p_kernel(p_ref, mu_ref, w0_ref, w2_ref, a0_ref, a2_ref, g2_ref, kk_ref, ka_ref, rk_ref,
                   lnw_ref, lnb_ref, y_ref, s_ref, prev_scr):
    cidx = pl.program_id(1)

    @pl.when(cidx == 0)
    def _():
        s_ref[...] = jnp.zeros_like(s_ref)
        prev_scr[...] = jnp.zeros_like(prev_scr)

    p = p_ref[...]
    n = p.shape[0]
    row1 = lax.broadcasted_iota(jnp.int32, (n, 1), 0)
    prev = jnp.where(row1 == 0, prev_scr[0:1, :], pltpu.roll(p, 1, 0))
    prev_scr[0:1, :] = p[n - 1:n, :]
    r, k2, v, lw, a, g, kkraw = _rwkv_tokens(p, prev, mu_ref[...], w0_ref[...], w2_ref[...], a0_ref[...],
                                             a2_ref[...], g2_ref[...], kk_ref[...], ka_ref[...])
    cum = _dot_sel_l(_tri_incl(n), lw)
    e_c = jnp.exp(cum)
    e_x = jnp.exp(cum - lw)
    e_n = jnp.exp(-cum)
    e_l = jnp.exp(cum[n - 1:n, :] - cum)
    rk = rk_ref[...]
    lnw = lnw_ref[...]
    lnb = lnb_ref[...]

    m0, row, col = _pair_masks(n)
    strict = col < row
    incl = col <= row
    eye = jnp.where(col == row, 1.0, 0.0)
    ms, _, _ = _pair_masks(RW_HEAD)

    prs = range(RW_HEADS // 2)
    sls = [slice(pi * LANES, (pi + 1) * LANES) for pi in prs]
    kkn = []
    for sl in sls:
        kkp = kkraw[:, sl]
        kkn.append(kkp / jnp.maximum(jnp.sqrt(_headsum_pair(kkp * kkp, m0)), 1e-12))
    bv = [kkn[i] * a[:, sls[i]] for i in prs]
    at = [-kkn[i] * e_x[:, sls[i]] for i in prs]
    rt = [r[:, sl] * e_c[:, sl] for sl in sls]
    vv = [v[:, sl] for sl in sls]
    ar = [jnp.concatenate([at[i], rt[i]], axis=0) for i in prs]
    sab = [_dot_hi(ar[i], _bd(bv[i] * e_n[:, sls[i]], m0), NT) for i in prs]
    sak = [_dgb(ar[i], _bd(k2[:, sls[i]] * e_n[:, sls[i]], m0), NT) for i in prs]
    a_ab = [jnp.where(strict, s[:n], 0.0) for s in sab]
    a_ak = [jnp.where(strict, s[:n], 0.0) for s in sak]
    m_rb = [jnp.where(incl, s[n:], 0.0) for s in sab]
    m_rk = [jnp.where(incl, s[n:], 0.0) for s in sak]
    tinv = [eye + x for x in a_ab]
    xs = a_ab
    akv = [_dgb(a_ak[i], _bd(vv[i], m0)) for i in prs]
    for _ in range(n.bit_length() - 2):
        xs = [_dot_hi(x, _bd(x, m0)) for x in xs]
        tinv = [tinv[i] + _dot_hi(tinv[i], _bd(xs[i], m0)) for i in prs]
    tw = [_dgb(tinv[i], jnp.concatenate([_bd(at[i], m0), _bd(akv[i], m0)], axis=1)) for i in prs]
    s0 = [s_ref[0, pi] for pi in prs]
    us = [_dgb(jnp.concatenate([tw[i][:, :LANES], rt[i]], axis=0), _bd(s0[i], ms), NT) for i in prs]
    u = [us[i][:n] + tw[i][:, LANES:] for i in prs]
    uv = [jnp.concatenate([u[i], vv[i]], axis=0) for i in prs]
    y = [us[i][n:] + _dgb(jnp.concatenate([m_rb[i], m_rk[i]], axis=1),
                             jnp.concatenate([_bd(u[i], m0), _bd(vv[i], m0)], axis=0)) for i in prs]
    for i in prs:
        sl = sls[i]
        bk = jnp.concatenate([bv[i] * e_l[:, sl], k2[:, sl] * e_l[:, sl]], axis=0)
        z = _dgb(uv[i], bk, TN)
        s_ref[0, i] = s0[i] * e_c[n - 1:n, sl] + jnp.where(ms, z[:RW_HEAD], z[RW_HEAD:])
    for i in prs:
        sl = sls[i]
        mean = _headsum_pair(y[i], m0) * (1.0 / RW_HEAD)
        d = y[i] - mean
        var = _headsum_pair(d * d, m0) * (1.0 / RW_HEAD)
        yn = d * lax.rsqrt(var + RW_GN_EPS) * lnw[:, sl] + lnb[:, sl]
        bonus = _headsum_pair(r[:, sl] * k2[:, sl] * rk[:, sl], m0) * vv[i]
        y_ref[0, :, sl] = ((yn + bonus) * g[:, sl]).astype(y_ref.dtype)


def _rwkv_weights(w2, a2, g2):
    lw = w2.shape[0]
    w2p = jnp.concatenate([w2, jnp.zeros((LANES - lw, BRANCH_W), F32)], axis=0).astype(BF16)
    a2p = jnp.concatenate([jnp.zeros((lw, BRANCH_W), F32), a2], axis=0).astype(BF16)
    g2p = jnp.pad(g2, ((0, RW_PAD - 3 * BRANCH_W - LANES - g2.shape[0]), (0, 0))).astype(BF16)
    return w2p, a2p, g2p


def _row(x):
    return x.reshape(1, -1)


def _const_spec(shape):
    nd = len(shape)
    return pl.BlockSpec(shape, lambda *_: (0,) * nd)


def _chained_call(kernel_fn, prev, inputs, in_specs, **kw):
    if prev is None:
        return pl.pallas_call(kernel_fn, in_specs=in_specs, **kw)(*inputs)
    n_in = len(inputs)

    def body(*refs):
        return kernel_fn(*refs[:n_in], *refs[n_in + 1:])

    return pl.pallas_call(body, in_specs=list(in_specs) + [pl.BlockSpec(memory_space=pl.ANY)],
                          input_output_aliases={n_in: 0}, **kw)(*inputs, prev)


def _put_kernel(rows_ref, o_ref):
    o_ref[...] = rows_ref[...].astype(o_ref.dtype)


def _put_rows(buf, rows, blk):
    g, n, c = rows.shape
    return _chained_call(
        _put_kernel, buf, (rows,), [_const_spec(rows.shape)],
        grid=(1,),
        out_specs=pl.BlockSpec((g, n, c), lambda i: (0, blk, 0)),
        out_shape=jax.ShapeDtypeStruct(buf.shape, buf.dtype),
        compiler_params=_cparams(("arbitrary",)),
        name="put_rows",
    )


def _rwkv_prompt(proj, nb, t, wts, y3, rows):
    nc = t // CHUNK
    consts = wts
    in_specs = [pl.BlockSpec((CHUNK, RW_PAD), lambda b, c: (b * nc + c, 0))]
    in_specs += [_const_spec(x.shape) for x in consts]
    return _chained_call(
        _rwkv_p_kernel, y3, (proj, *consts), in_specs,
        grid=(nb, nc),
        out_specs=[pl.BlockSpec((1, CHUNK, BRANCH_W), lambda b, c: (0, b * nc + c, 0)),
                   pl.BlockSpec((1, RW_HEADS // 2, RW_HEAD, LANES), lambda b, c: (b, 0, 0, 0))],
        out_shape=[jax.ShapeDtypeStruct((N_BRANCH, rows, BRANCH_W), BF16),
                   jax.ShapeDtypeStruct((nb, RW_HEADS // 2, RW_HEAD, LANES), F32)],
        scratch_shapes=[pltpu.VMEM((8, RW_PAD), F32)],
        compiler_params=_cparams(("parallel", "arbitrary")),
        name="rwkv_prompt",
    )


def _unpack_rwkv_state(sp):
    nb = sp.shape[0]
    s = sp.reshape(nb, RW_HEADS // 2, RW_HEAD, 2, RW_HEAD)
    return jnp.transpose(s, (0, 1, 3, 2, 4)).reshape(nb, RW_HEADS, RW_HEAD, RW_HEAD)


def _rwkv_s_prep_kernel(p_ref, prev_ref, mu_ref, w0_ref, w2_ref, a0_ref, a2_ref, g2_ref, kk_ref, ka_ref, rk_ref,
                        o_ref):
    r, k2, v, lw, a, g, kkraw = _rwkv_tokens(p_ref[...], prev_ref[...], mu_ref[...], w0_ref[...], w2_ref[...],
                                             a0_ref[...], a2_ref[...], g2_ref[...], kk_ref[...], ka_ref[...])
    n = r.shape[0]
    m0, _, _ = _pair_masks(n)
    rk = rk_ref[...]
    o_ref[0] = r
    o_ref[1] = jnp.exp(lw)
    o_ref[2] = k2
    o_ref[3] = v
    o_ref[6] = g
    for pi in range(RW_HEADS // 2):
        sl = slice(pi * LANES, (pi + 1) * LANES)
        kkp = kkraw[:, sl]
        kkn = kkp / jnp.maximum(jnp.sqrt(_headsum_pair(kkp * kkp, m0)), 1e-12)
        o_ref[4, :, sl] = -kkn
        o_ref[5, :, sl] = kkn * a[:, sl]
        o_ref[7, :, sl] = _headsum_pair(r[:, sl] * k2[:, sl] * rk[:, sl], m0) * v[:, sl]


def _rwkv_s_kernel(vec_ref, s_ref, lnw_ref, lnb_ref, so_ref, y_ref, y_scr):
    r = vec_ref[0, 0]
    w = vec_ref[1, 0]
    k = vec_ref[2, 0]
    a = vec_ref[4, 0]
    b = vec_ref[5, 0]

    def body(i, carry):
        s = s_ref[0, 0, i]
        sa = jnp.sum(s * a, axis=0, keepdims=True)
        sn = s * w + sa * b + vec_ref[3, 0, pl.ds(i, 1), :] * k
        so_ref[0, 0, i] = sn
        y_scr[pl.ds(i, 1), :] = jnp.sum(sn * r, axis=0, keepdims=True)
        return carry

    lax.fori_loop(0, RW_HEAD, body, 0)
    y = y_scr[...]
    d = y - jnp.mean(y, axis=0, keepdims=True)
    var = jnp.mean(d * d, axis=0, keepdims=True)
    yn = d * lax.rsqrt(var + RW_GN_EPS) * lnw_ref[0] + lnb_ref[0]
    y_ref[0] = (yn + vec_ref[7, 0]) * vec_ref[6, 0]


def _rwkv_sample(proj, row0_blk, shift_prev, state_t, layer, wts, lnw, lnb, s_all):
    nb = shift_prev.shape[0]
    consts = wts
    vec = pl.pallas_call(
        _rwkv_s_prep_kernel,
        grid=(1,),
        in_specs=[pl.BlockSpec((nb, RW_PAD), lambda i: (row0_blk, 0)),
                  pl.BlockSpec((nb, RW_PAD), lambda i: (0, 0))] + [_const_spec(x.shape) for x in consts],
        out_specs=pl.BlockSpec((8, nb, BRANCH_W), lambda i: (0, 0, 0)),
        out_shape=jax.ShapeDtypeStruct((8, nb, BRANCH_W), F32),
        compiler_params=_cparams(("arbitrary",)),
        name="rwkv_sample_prep",
    )(proj, shift_prev, *consts)
    vec_t = jnp.transpose(vec, (0, 2, 1)).reshape(8, RW_HEADS, RW_HEAD, nb)
    ln_shape = (RW_HEADS, RW_HEAD, nb)
    lnw_t = jnp.broadcast_to(lnw.reshape(RW_HEADS, RW_HEAD, 1), ln_shape)
    lnb_t = jnp.broadcast_to(lnb.reshape(RW_HEADS, RW_HEAD, 1), ln_shape)
    st_spec = pl.BlockSpec((1, 1, RW_HEAD, RW_HEAD, nb), lambda h: (layer, h, 0, 0, 0))
    ch_spec = pl.BlockSpec((1, RW_HEAD, nb), lambda h: (h, 0, 0))
    s_all, y_t = _chained_call(
        _rwkv_s_kernel, s_all, (vec_t, state_t, lnw_t, lnb_t),
        [pl.BlockSpec((8, 1, RW_HEAD, nb), lambda h: (0, h, 0, 0)), st_spec, ch_spec, ch_spec],
        grid=(RW_HEADS,),
        out_specs=[st_spec, ch_spec],
        out_shape=[jax.ShapeDtypeStruct(state_t.shape, F32), jax.ShapeDtypeStruct(ln_shape, F32)],
        scratch_shapes=[pltpu.VMEM((RW_HEAD, nb), F32)],
        compiler_params=_cparams(("parallel",)),
        name="rwkv_sample",
    )
    return jnp.transpose(y_t.reshape(BRANCH_W, nb)), s_all


def _expand_heads():
    k = lax.broadcasted_iota(jnp.int32, (LANES, BRANCH_W), 0)
    c = lax.broadcasted_iota(jnp.int32, (LANES, BRANCH_W), 1)
    return jnp.where(jnp.right_shift(c, 6) == k, 1.0, 0.0).astype(BF16)


def _ssd_p_kernel(p_ref, cw_ref, cb_ref, dtb_ref, alog_ref, dsk_ref, nw_ref, y_ref, st_ref, buf_scr):
    cidx = pl.program_id(1)
    n = p_ref.shape[0]

    @pl.when(cidx == 0)
    def _():
        st_ref[...] = jnp.zeros_like(st_ref)
        buf_scr[0:8, :] = jnp.zeros((8, SSD_CONV_DIM), F32)

    z = p_ref[:, 0:BRANCH_W]
    buf_scr[8:8 + n, :] = p_ref[:, BRANCH_W:BRANCH_W + SSD_CONV_DIM]
    conv = cb_ref[...]
    for i in range(SSD_CONV):
        conv = conv + cw_ref[i:i + 1, :] * buf_scr[pl.ds(8 - (SSD_CONV - 1) + i, n), :]
    buf_scr[0:8, :] = buf_scr[n:n + 8, :]
    xa = _silu(conv)
    xs = xa[:, 0:BRANCH_W]
    bm = xa[:, BRANCH_W:BRANCH_W + SSD_GROUPS * SSD_STATE]
    cm = xa[:, BRANCH_W + SSD_GROUPS * SSD_STATE:]
    dt = _softplus(p_ref[:, BRANCH_W + SSD_CONV_DIM:SSD_PAD] + dtb_ref[...])
    dte = _dot_sel_r(dt, _expand_heads())
    da = dte * (-jnp.exp(alog_ref[...]))
    cum = _dot_sel_l(_tri_incl(n), da)
    xdt = xs * dte
    ecum = jnp.exp(cum)
    cl = cum[n - 1:n, :]
    xdl = xdt * jnp.exp(cl - cum)
    pl_ = jnp.exp(cl)

    m0, row, col = _pair_masks(n)
    incl = col <= row
    eye = col == row
    gw = BRANCH_W // SSD_GROUPS
    ppg = gw // LANES
    ys = []
    for gi in range(SSD_GROUPS):
        gs = slice(gi * gw, (gi + 1) * gw)
        bg = bm[:, gi * SSD_STATE:(gi + 1) * SSD_STATE]
        cg = cm[:, gi * SSD_STATE:(gi + 1) * SSD_STATE]
        cbp = _dgb(cg, jnp.concatenate([bg, bg], axis=0), NT)
        st = st_ref[0, :, gs]
        cs = _dgb(cg, st)
        for q in range(ppg):
            sl = slice(gi * gw + q * LANES, gi * gw + (q + 1) * LANES)
            cp = cum[:, sl]
            rp = jnp.sum(jnp.where(eye, cp, 0.0), axis=0, keepdims=True)
            seg = jnp.exp(jnp.where(incl, cp - rp, -jnp.inf))
            yp = _dgb(cbp * seg, _bd(xdt[:, sl], m0)) + ecum[:, sl] * cs[:, q * LANES:(q + 1) * LANES]
            ys.append(yp)
        st_ref[0, :, gs] = st * pl_[:, gs] + _dgb(bg, xdl[:, gs], TN)
    y = jnp.concatenate(ys, axis=1) + xs * dsk_ref[...]
    y = y * _silu(z)
    for gi in range(SSD_GROUPS):
        gs = slice(gi * gw, (gi + 1) * gw)
        yg = y[:, gs]
        yg = yg * lax.rsqrt(jnp.mean(yg * yg, axis=-1, keepdims=True) + GROUP_EPS)
        y_ref[0, :, gs] = (yg * nw_ref[:, gs]).astype(y_ref.dtype)


def _ssd_consts(conv_w, conv_b, dt_bias, a_log, d_skip, norm_w):
    dtb = jnp.pad(dt_bias, (0, LANES - SSD_HEADS)).reshape(1, LANES)
    return (conv_w, _row(conv_b), dtb, _row(jnp.repeat(a_log, SSD_HEAD)), _row(jnp.repeat(d_skip, SSD_HEAD)),
            _row(norm_w))


def _ssd_prompt(proj, nb, t, consts, y3):
    nc = t // CHUNK
    return _chained_call(
        _ssd_p_kernel, y3, (proj, *consts),
        [pl.BlockSpec((CHUNK, SSD_PAD), lambda b, c: (b * nc + c, 0))] + [_const_spec(x.shape) for x in consts],
        grid=(nb, nc),
        out_specs=[pl.BlockSpec((1, CHUNK, BRANCH_W), lambda b, c: (1, b * nc + c, 0)),
                   pl.BlockSpec((1, SSD_STATE, BRANCH_W), lambda b, c: (b, 0, 0))],
        out_shape=[jax.ShapeDtypeStruct(y3.shape, BF16),
                   jax.ShapeDtypeStruct((nb, SSD_STATE, BRANCH_W), F32)],
        scratch_shapes=[pltpu.VMEM((CHUNK + 8, SSD_CONV_DIM), F32)],
        compiler_params=_cparams(("parallel", "arbitrary")),
        name="ssd_prompt",
    )


def _unpack_ssd_state(st):
    nb = st.shape[0]
    return jnp.transpose(st.reshape(nb, SSD_STATE, SSD_HEADS, SSD_HEAD), (0, 2, 3, 1))


def _ssd_s_prep_kernel(p_ref, cv_ref, cw_ref, cb_ref, dtb_ref, alog_ref, dsk_ref, o_ref, bc_ref):
    z = p_ref[:, 0:BRANCH_W]
    conv = cb_ref[...] + cw_ref[SSD_CONV - 1:SSD_CONV, :] * p_ref[:, BRANCH_W:BRANCH_W + SSD_CONV_DIM]
    for i in range(SSD_CONV - 1):
        conv = conv + cw_ref[i:i + 1, :] * cv_ref[i]
    xa = _silu(conv)
    xs = xa[:, 0:BRANCH_W]
    dt = _softplus(p_ref[:, BRANCH_W + SSD_CONV_DIM:SSD_PAD] + dtb_ref[...])
    dte = _dot_sel_r(dt, _expand_heads())
    o_ref[0] = xs * dte
    o_ref[1] = jnp.exp(dte * (-jnp.exp(alog_ref[...])))
    o_ref[2] = xs * dsk_ref[...]
    o_ref[3] = _silu(z)
    bc_ref[...] = xa[:, BRANCH_W:]


def _ssd_s_kernel(vec_ref, bc_ref, s_ref, nw_ref, so_ref, y_ref):
    nbb = s_ref.shape[1]
    n = SSD_HEAD
    eye = lax.broadcasted_iota(jnp.int32, (n, n), 0) == lax.broadcasted_iota(jnp.int32, (n, n), 1)
    hpg = SSD_HEADS // SSD_GROUPS

    def body(bi, carry):
        for h in range(SSD_HEADS):
            hs = pl.ds(h, 1)
            gi = h // hpg
            s = s_ref[0, bi, h]
            xdt_ = vec_ref[0, bi, hs, :]
            dec = vec_ref[1, bi, hs, 0:1]
            b_ = bc_ref[bi, pl.ds(gi, 1), :]
            c_ = bc_ref[bi, pl.ds(SSD_GROUPS + gi, 1), :]
            xc = jnp.sum(jnp.where(eye, xdt_, 0.0), axis=-1, keepdims=True)
            sn = s * dec + xc * b_
            so_ref[0, bi, h] = sn
            yc = jnp.sum(sn * c_, axis=-1, keepdims=True)
            yr = jnp.sum(jnp.where(eye, yc, 0.0), axis=0, keepdims=True)
            y_ref[bi, hs, :] = (yr + vec_ref[2, bi, hs, :]) * vec_ref[3, bi, hs, :]
        for gi in range(SSD_GROUPS):
            rs = pl.ds(gi * hpg, hpg)
            yg = y_ref[bi, rs, :]
            ms = jnp.sum(jnp.sum(yg * yg, axis=-1, keepdims=True), axis=0, keepdims=True) * (1.0 / (hpg * n))
            y_ref[bi, rs, :] = yg * lax.rsqrt(ms + GROUP_EPS) * nw_ref[rs, :]
        return carry

    lax.fori_loop(0, nbb, body, 0)


def _ssd_sample(proj, row0_blk, conv_prev, state, layer, consts, s_all):
    nb = conv_prev.shape[1]
    cw, cb, dtb, alog, dsk, nw = consts
    vec, bc = pl.pallas_call(
        _ssd_s_prep_kernel,
        grid=(1,),
        in_specs=[pl.BlockSpec((nb, SSD_PAD), lambda i: (row0_blk, 0)),
                  _const_spec(conv_prev.shape)] + [_const_spec(x.shape) for x in (cw, cb, dtb, alog, dsk)],
        out_specs=[_const_spec((4, nb, BRANCH_W)), _const_spec((nb, 2 * SSD_GROUPS * SSD_STATE))],
        out_shape=[jax.ShapeDtypeStruct((4, nb, BRANCH_W), F32),
                   jax.ShapeDtypeStruct((nb, 2 * SSD_GROUPS * SSD_STATE), F32)],
        compiler_params=_cparams(("arbitrary",)),
        name="ssd_sample_prep",
    )(proj, conv_prev, cw, cb, dtb, alog, dsk)
    vec = vec.reshape(4, nb, SSD_HEADS, SSD_HEAD)
    bc = bc.reshape(nb, 2 * SSD_GROUPS, SSD_STATE)
    bb = 8
    st_spec = pl.BlockSpec((1, bb, SSD_HEADS, SSD_HEAD, SSD_STATE), lambda i: (layer, i, 0, 0, 0))
    s_all, y = _chained_call(
        _ssd_s_kernel, s_all, (vec, bc, state, nw.reshape(SSD_HEADS, SSD_HEAD)),
        [pl.BlockSpec((4, bb, SSD_HEADS, SSD_HEAD), lambda i: (0, i, 0, 0)),
         pl.BlockSpec((bb, 2 * SSD_GROUPS, SSD_STATE), lambda i: (i, 0, 0)), st_spec,
         _const_spec((SSD_HEADS, SSD_HEAD))],
        grid=(nb // bb,),
        out_specs=[st_spec, pl.BlockSpec((bb, SSD_HEADS, SSD_HEAD), lambda i: (i, 0, 0))],
        out_shape=[jax.ShapeDtypeStruct(state.shape, F32),
                   jax.ShapeDtypeStruct((nb, SSD_HEADS, SSD_HEAD), F32)],
        compiler_params=_cparams(("parallel",)),
        name="ssd_sample",
    )
    return y.reshape(nb, BRANCH_W), s_all


def _gla_tokens(p, au, ab):
    q = p[:, 0:GLA_DK] * (GLA_HK ** -0.5)
    k = p[:, GLA_DK:2 * GLA_DK]
    v = p[:, 2 * GLA_DK:2 * GLA_DK + GLA_DV]
    r = p[:, 2 * GLA_DK + GLA_DV:2 * GLA_DK + 2 * GLA_DV]
    ad = p[:, 2 * GLA_DK + 2 * GLA_DV:GLA_PAD]
    lg = -_softplus(-(_dgb(ad, au) + ab)) * (1.0 / GLA_TAU)
    return q, k, v, r, lg


def _gla_p_kernel(p_ref, au_ref, ab_ref, nw_ref, o_ref, st_ref):
    cidx = pl.program_id(1)

    @pl.when(cidx == 0)
    def _():
        st_ref[...] = jnp.zeros_like(st_ref)

    n = p_ref.shape[0]
    q, k, v, r, lg = _gla_tokens(p_ref[...], au_ref[...], ab_ref[...])
    cum = _dot_sel_l(_tri_incl(n), lg)
    cumx = cum - lg
    cl = cum[n - 1:n, :]
    qe = q * jnp.exp(cum)
    kb = k * jnp.exp(cl - cum)
    pl_ = jnp.exp(cl)
    rowi = lax.broadcasted_iota(jnp.int32, (SUB, 1), 0)
    for h in range(GLA_HEADS):
        sk = slice(h * GLA_HK, (h + 1) * GLA_HK)
        sv = slice(h * GLA_HV, (h + 1) * GLA_HV)
        st = st_ref[0, h]
        vh = v[:, sv]
        parts = []
        for blk in range(n // SUB):
            lo = blk * SUB
            rs = slice(lo, lo + SUB)
            q_i = q[rs, sk]
            c_i = cum[rs, sk]
            k_i = k[rs, sk]
            v_i = vh[rs]
            if blk > 0:
                cref = cumx[lo:lo + 1, sk]
                qt = q_i * jnp.exp(c_i - cref)
                kt = k[0:lo, sk] * jnp.exp(cref - cum[0:lo, sk])
                acc = _dgb(_dgb(qt, kt, NT), vh[0:lo])
            else:
                acc = jnp.zeros((SUB, GLA_HV), F32)
            for s in range(SUB):
                e = jnp.exp(c_i - c_i[s:s + 1])
                w = jnp.sum(q_i * k_i[s:s + 1] * e, axis=-1, keepdims=True)
                acc = acc + jnp.where(rowi >= s, w, 0.0) * v_i[s:s + 1]
            parts.append(acc)
        o = jnp.concatenate(parts, axis=0) + _dgb(qe[:, sk], st, NT)
        st_ref[0, h] = st * pl_[:, sk] + _dgb(vh, kb[:, sk], TN)
        o = o * lax.rsqrt(jnp.mean(o * o, axis=-1, keepdims=True) + GROUP_EPS) * nw_ref[...]
        o_ref[0, :, sv] = (o * _silu(r[:, sv])).astype(o_ref.dtype)


def _gla_consts(alpha_up, alpha_b, norm_w):
    au = jnp.pad(alpha_up, ((0, LANES - GLA_LORA), (0, 0))).astype(BF16)
    return au, _row(alpha_b), _row(norm_w)


def _gla_prompt(proj, nb, t, consts, y3):
    nc = t // CHUNK
    return _chained_call(
        _gla_p_kernel, y3, (proj, *consts),
        [pl.BlockSpec((CHUNK, GLA_PAD), lambda b, c: (b * nc + c, 0))] + [_const_spec(x.shape) for x in consts],
        grid=(nb, nc),
        out_specs=[pl.BlockSpec((1, CHUNK, GLA_DV), lambda b, c: (2, b * nc + c, 0)),
                   pl.BlockSpec((1, GLA_HEADS, GLA_HV, GLA_HK), lambda b, c: (b, 0, 0, 0))],
        out_shape=[jax.ShapeDtypeStruct(y3.shape, BF16),
                   jax.ShapeDtypeStruct((nb, GLA_HEADS, GLA_HV, GLA_HK), F32)],
        compiler_params=_cparams(("parallel", "arbitrary")),
        name="gla_prompt",
    )


def _gla_s_prep_kernel(p_ref, au_ref, ab_ref, qk_ref, vr_ref):
    q, k, v, r, lg = _gla_tokens(p_ref[...], au_ref[...], ab_ref[...])
    qk_ref[0] = q
    qk_ref[1] = k
    qk_ref[2] = jnp.exp(lg)
    vr_ref[0] = v
    vr_ref[1] = _silu(r)


def _gla_s_kernel(qk_ref, vr_ref, s_ref, nw_ref, so_ref, y_ref):
    nbb = s_ref.shape[1]
    n = GLA_HK
    eye = lax.broadcasted_iota(jnp.int32, (n, n), 0) == lax.broadcasted_iota(jnp.int32, (n, n), 1)

    def col(x):
        return jnp.sum(jnp.where(eye, x, 0.0), axis=-1, keepdims=True)

    def body(bi, carry):
        for h in range(GLA_HEADS):
            hs = pl.ds(h, 1)
            s = s_ref[0, bi, h]
            sn = s * col(qk_ref[2, bi, hs, :]) + col(qk_ref[1, bi, hs, :]) * vr_ref[0, bi, hs, :]
            so_ref[0, bi, h] = sn
            o = jnp.sum(sn * col(qk_ref[0, bi, hs, :]), axis=0, keepdims=True)
            o = o * lax.rsqrt(jnp.mean(o * o, axis=-1, keepdims=True) + GROUP_EPS) * nw_ref[...]
            y_ref[bi, hs, :] = o * vr_ref[1, bi, hs, :]
        return carry

    lax.fori_loop(0, nbb, body, 0)


def _gla_sample(proj, row0_blk, nb, state, layer, consts, s_all):
    au, ab, nw = consts
    qk, vr = pl.pallas_call(
        _gla_s_prep_kernel,
        grid=(1,),
        in_specs=[pl.BlockSpec((nb, GLA_PAD), lambda i: (row0_blk, 0)), _const_spec(au.shape),
                  _const_spec(ab.shape)],
        out_specs=[_const_spec((3, nb, GLA_DK)), _const_spec((2, nb, GLA_DV))],
        out_shape=[jax.ShapeDtypeStruct((3, nb, GLA_DK), F32), jax.ShapeDtypeStruct((2, nb, GLA_DV), F32)],
        compiler_params=_cparams(("arbitrary",)),
        name="gla_sample_prep",
    )(proj, au, ab)
    qk = qk.reshape(3, nb, GLA_HEADS, GLA_HK)
    vr = vr.reshape(2, nb, GLA_HEADS, GLA_HV)
    bb = 8
    st_spec = pl.BlockSpec((1, bb, GLA_HEADS, GLA_HK, GLA_HV), lambda i: (layer, i, 0, 0, 0))
    s_all, y = _chained_call(
        _gla_s_kernel, s_all, (qk, vr, state, nw),
        [pl.BlockSpec((3, bb, GLA_HEADS, GLA_HK), lambda i: (0, i, 0, 0)),
         pl.BlockSpec((2, bb, GLA_HEADS, GLA_HV), lambda i: (0, i, 0, 0)), st_spec, _const_spec(nw.shape)],
        grid=(nb // bb,),
        out_specs=[st_spec, pl.BlockSpec((bb, GLA_HEADS, GLA_HV), lambda i: (i, 0, 0))],
        out_shape=[jax.ShapeDtypeStruct(state.shape, F32),
                   jax.ShapeDtypeStruct((nb, GLA_HEADS, GLA_HV), F32)],
        compiler_params=_cparams(("parallel",)),
        name="gla_sample",
    )
    return y.reshape(nb, GLA_DV), s_all


def kernel(x_prompt, x_sample, state_rwkv, state_rwkv_shift, state_ssd, state_ssd_conv, state_gla, cache_mem_k, cache_mem_v, mem_prompt, norm_mix_pre, norm_mix_post, norm_x_pre, norm_x_post, norm_ffn_pre, norm_ffn_post, w_in, rw_mu, rw_w0, rw_w2, rw_a0, rw_a2, rw_g2, rw_kk, rw_ka, rw_rk, rw_ln_w, rw_ln_b, ssd_conv_w, ssd_conv_b, ssd_dt_bias, ssd_a_log, ssd_d, ssd_norm_w, gla_alpha_up, gla_alpha_b, gla_norm_w, w_branch, w_out, x_mem_norm, x_wq, x_wk, x_wv, x_wo, ffn_up, ffn_down):
    nbp, t, d = x_prompt.shape
    nbs = x_sample.shape[0]
    depth = w_in.shape[0]
    mp = nbp * t
    m = mp + nbs
    sblk = mp // nbs

    x = jnp.concatenate([x_prompt.reshape(mp, d), x_sample.reshape(nbs, d)], axis=0)
    mem_rows = mem_prompt.reshape(nbp * N_MEM, d)
    tm_norm = _row_tile(m, (1040, 1024, 512, 256, 128, 64, 8))

    w_rw_all, w_ssd_all, w_gla_all, w_gate_all = _split_w_in(w_in)
    wb_all = _cast_bf16(w_branch.reshape(depth * N_BRANCH, BRANCH_W, d)).reshape(depth, N_BRANCH, BRANCH_W, d)
    w_out_b, wq_b, wk_b, wv_b, wo_b, up_b, down_b = [_cast_bf16(w) for w in
                                                     (w_out, x_wq, x_wk, x_wv, x_wo, ffn_up, ffn_down)]

    xn = _norm_rows(x, norm_mix_pre[0], tm_norm)
    p_acc = [[] for _ in range(7)]
    s_small = [[] for _ in range(2)]
    rw_ss = ssd_ss = gla_ss = None
    state_rwkv_t = jnp.transpose(state_rwkv, (0, 2, 3, 4, 1))
    for l in range(depth):
        proj_rw = _mm(xn, w_rw_all, l, name="proj_rwkv")
        proj_ssd = _mm(xn, w_ssd_all, l, name="proj_ssd")
        proj_gla = _mm(xn, w_gla_all, l, name="proj_gla")
        gate = _mm(xn, w_gate_all, l, out_dtype=BF16, act="sigmoid", name="proj_gate")

        w2p, a2p, g2p = _rwkv_weights(rw_w2[l], rw_a2[l], rw_g2[l])
        rw_tok = (_row(jnp.pad(rw_mu[l], (0, RW_PAD - RW_PROJ))), _row(rw_w0[l]), w2p, _row(rw_a0[l]), a2p, g2p,
                  _row(rw_kk[l]), _row(rw_ka[l]), _row(rw_rk[l]))
        y3, rw_sp = _rwkv_prompt(proj_rw, nbp, t, rw_tok + (_row(rw_ln_w[l]), _row(rw_ln_b[l])), None, m)
        shift_prev = jnp.pad(state_rwkv_shift[l], ((0, 0), (0, RW_PAD - RW_PROJ)))
        ya_s, rw_ss = _rwkv_sample(proj_rw, sblk, shift_prev, state_rwkv_t, l, rw_tok, rw_ln_w[l], rw_ln_b[l], rw_ss)

        ssd_c = _ssd_consts(ssd_conv_w[l], ssd_conv_b[l], ssd_dt_bias[l], ssd_a_log[l], ssd_d[l], ssd_norm_w[l])
        y3, ssd_sp = _ssd_prompt(proj_ssd, nbp, t, ssd_c, y3)
        conv_prev = jnp.transpose(state_ssd_conv[l], (1, 0, 2))
        yb_s, ssd_ss = _ssd_sample(proj_ssd, sblk, conv_prev, state_ssd, l, ssd_c, ssd_ss)

        gla_c = _gla_consts(gla_alpha_up[l], gla_alpha_b[l], gla_norm_w[l])
        y3, gla_sp = _gla_prompt(proj_gla, nbp, t, gla_c, y3)
        yc_s, gla_ss = _gla_sample(proj_gla, sblk, nbs, state_gla, l, gla_c, gla_ss)

        y3 = _put_rows(y3, jnp.stack([ya_s, yb_s, yc_s]), sblk)
        mix = _merge(y3, wb_all, l, gate)
        x, xn = _mm_resnorm(mix, w_out_b, l, x, norm_mix_post[l], norm_x_pre[l], name="out_proj")

        mn = _norm_rows(mem_rows, x_mem_norm[l], _row_tile(mem_rows.shape[0], (1024, 512, 256, 128, 64, 8)))
        mk = _mm(mn, wk_b, l, name="mem_k")
        mv = _mm(mn, wv_b, l, name="mem_v")
        q = _mm(xn, wq_b, l, out_dtype=BF16, name="attn_q")
        o = _attn_prompt(q, mk.reshape(nbp, N_MEM, d), mv.reshape(nbp, N_MEM, d), nbp, t)
        o_s = _attn_sample(q[mp:].reshape(nbs, X_HEADS, X_HEAD), cache_mem_k, cache_mem_v, l)
        o = _put_rows(o.reshape(1, m, d), o_s.reshape(1, nbs, d), sblk).reshape(m, d)
        x, xn = _mm_resnorm(o, wo_b, l, x, norm_x_post[l], norm_ffn_pre[l], name="attn_out")

        hf = _mm(xn, up_b, l, out_dtype=BF16, act="relu2", name="ffn_up")
        g_next = norm_mix_pre[l + 1] if l + 1 < depth else norm_mix_pre[l]
        x, xn = _mm_resnorm(hf, down_b, l, x, norm_ffn_post[l], g_next, name="ffn_down")

        last = [b * t + t - 1 for b in range(nbp)]
        rw_shift_p = jnp.stack([proj_rw[i, :RW_PROJ] for i in last])
        ssd_conv_p = jnp.stack([proj_ssd[i - (SSD_CONV - 2):i + 1, BRANCH_W:BRANCH_W + SSD_CONV_DIM] for i in last])
        new_p = (_unpack_rwkv_state(rw_sp), rw_shift_p, _unpack_ssd_state(ssd_sp), ssd_conv_p,
                 jnp.transpose(gla_sp, (0, 1, 3, 2)),
                 mk.reshape(nbp, N_MEM, X_HEADS, X_HEAD), mv.reshape(nbp, N_MEM, X_HEADS, X_HEAD))
        xbc_s = proj_ssd[mp:, BRANCH_W:BRANCH_W + SSD_CONV_DIM]
        new_s = (proj_rw[mp:, :RW_PROJ],
                 jnp.concatenate([state_ssd_conv[l][:, 1:], xbc_s[:, None, :]], axis=1))
        for acc, val in zip(p_acc, new_p):
            acc.append(val)
        for acc, val in zip(s_small, new_s):
            acc.append(val)

    outs_p = [jnp.stack(a) for a in p_acc]
    s_shift, s_conv = [jnp.stack(a) for a in s_small]
    return (x[:mp].reshape(nbp, t, d), x[mp:].reshape(nbs, 1, d), *outs_p,
            jnp.transpose(rw_ss, (0, 4, 1, 2, 3)), s_shift, ssd_ss, s_conv, gla_ss)
```

```python
import functools

import jax
import jax.numpy as jnp
from jax import lax
from jax.experimental import pallas as pl
from jax.experimental.pallas import tpu as pltpu

F32 = jnp.float32
BF16 = jnp.bfloat16

D_MODEL = 2048
BRANCH_W = 1024
N_BRANCH = 3
RW_HEADS = 16
RW_HEAD = 64
RW_PROJ = 3360
RW_PAD = 3456
RW_GN_EPS = 64e-5
SSD_HEADS = 16
SSD_HEAD = 64
SSD_GROUPS = 2
SSD_STATE = 128
SSD_CONV = 4
SSD_CONV_DIM = 1536
SSD_PROJ = 2576
SSD_PAD = 2688
GLA_HEADS = 4
GLA_DK = 512
GLA_DV = 1024
GLA_HK = 128
GLA_HV = 256
GLA_LORA = 16
GLA_TAU = 16.0
GLA_PROJ = 3088
GLA_PAD = 3200
N_MEM = 256
X_HEADS = 4
X_HEAD = 512
D_FF = 8192
NORM_EPS = 1e-6
GROUP_EPS = 1e-5
CHUNK = 64
SUB = 16
LANES = 128
VMEM_LIMIT = 56 * 1024 * 1024

NN = ((1,), (0,))
NT = ((1,), (1,))
TN = ((0,), (0,))


def _cparams(sem):
    return pltpu.CompilerParams(dimension_semantics=sem, vmem_limit_bytes=VMEM_LIMIT)


def _dg(a, b, dims=NN):
    return lax.dot_general(a, b, (dims, ((), ())), preferred_element_type=F32)


def _dgb(a, b, dims=NN):
    return _dg(a.astype(BF16), b.astype(BF16), dims)


def _hl(x):
    h = x.astype(BF16)
    return h, (x - h.astype(F32)).astype(BF16)


def _dot_hi(a, b, dims=NN):
    ah, al = _hl(a)
    bh, bl = _hl(b)
    return _dg(ah, bh, dims) + (_dg(ah, bl, dims) + _dg(al, bh, dims))


def _split3(x):
    h = x.astype(BF16)
    r = x - h.astype(F32)
    m = r.astype(BF16)
    return h, m, (r - m.astype(F32)).astype(BF16)


def _dot_sel_l(sel, x):
    h, m, l = _split3(x)
    return _dg(sel, h) + (_dg(sel, m) + _dg(sel, l))


def _dot_sel_r(x, sel):
    h, m, l = _split3(x)
    return _dg(h, sel) + (_dg(m, sel) + _dg(l, sel))


def _softplus(x):
    return jnp.maximum(x, 0.0) + jnp.log1p(jnp.exp(-jnp.abs(x)))


def _sigmoid(x):
    return 1.0 / (1.0 + jnp.exp(-x))


def _silu(x):
    return x * _sigmoid(x)


def _tri_incl(n):
    r = lax.broadcasted_iota(jnp.int32, (n, n), 0)
    c = lax.broadcasted_iota(jnp.int32, (n, n), 1)
    return jnp.where(c <= r, 1.0, 0.0).astype(BF16)


def _pair_masks(rows):
    lane = lax.broadcasted_iota(jnp.int32, (rows, LANES), 1)
    row = lax.broadcasted_iota(jnp.int32, (rows, LANES), 0)
    m0 = lane < 64
    col = jnp.bitwise_and(lane, 63)
    return m0, row, col


def _headsum_pair(x, m0):
    s0 = jnp.sum(jnp.where(m0, x, 0.0), axis=-1, keepdims=True)
    s1 = jnp.sum(jnp.where(m0, 0.0, x), axis=-1, keepdims=True)
    return jnp.where(m0, s0, s1)


def _bd(x, m0):
    return jnp.concatenate([jnp.where(m0, x, 0.0), jnp.where(m0, 0.0, x)], axis=0)


def _norm_kernel(x_ref, g_ref, o_ref):
    x = x_ref[...]
    y = x * lax.rsqrt(jnp.mean(x * x, axis=-1, keepdims=True) + NORM_EPS)
    o_ref[...] = (y * g_ref[...]).astype(o_ref.dtype)


def _norm_rows(x, g, tm):
    m, d = x.shape
    return pl.pallas_call(
        _norm_kernel,
        grid=(m // tm,),
        in_specs=[pl.BlockSpec((tm, d), lambda i: (i, 0)), pl.BlockSpec((1, d), lambda i: (0, 0))],
        out_specs=pl.BlockSpec((tm, d), lambda i: (i, 0)),
        out_shape=jax.ShapeDtypeStruct((m, d), BF16),
        compiler_params=_cparams(("parallel",)),
        name="rmsnorm",
    )(x, g.reshape(1, d))


def _mm_kernel(a_ref, w_ref, o_ref, *scratch, nk, act):
    part = _dg(a_ref[...], w_ref[0])

    def finish(acc):
        if act == "relu2":
            acc = jnp.square(jnp.maximum(acc, 0.0))
        elif act == "sigmoid":
            acc = _sigmoid(acc)
        o_ref[...] = acc.astype(o_ref.dtype)

    if nk == 1:
        finish(part)
        return
    acc_ref, = scratch
    k = pl.program_id(2)

    @pl.when(k == 0)
    def _():
        acc_ref[...] = part

    @pl.when(k > 0)
    def _():
        acc_ref[...] += part

    @pl.when(k == nk - 1)
    def _():
        finish(acc_ref[...])


def _mm_resnorm_kernel(a_ref, w_ref, res_ref, gp_ref, gn_ref, x_ref, xn_ref, *scratch, nk):
    part = _dg(a_ref[...], w_ref[0])

    def finish(acc):
        y = acc * lax.rsqrt(jnp.mean(acc * acc, axis=-1, keepdims=True) + NORM_EPS) * gp_ref[...]
        x = res_ref[...] + y
        x_ref[...] = x
        xn = x * lax.rsqrt(jnp.mean(x * x, axis=-1, keepdims=True) + NORM_EPS) * gn_ref[...]
        xn_ref[...] = xn.astype(xn_ref.dtype)

    if nk == 1:
        finish(part)
        return
    acc_ref, = scratch
    k = pl.program_id(1)

    @pl.when(k == 0)
    def _():
        acc_ref[...] = part

    @pl.when(k > 0)
    def _():
        acc_ref[...] += part

    @pl.when(k == nk - 1)
    def _():
        finish(acc_ref[...])


def _pick(n, cands):
    for c in cands:
        if n % c == 0:
            return c
    return n


def _row_tile(m, cands):
    return _pick(m, cands)


def _mm(a, w, layer, out_dtype=F32, act=None, name="mm"):
    m, kd = a.shape
    n = w.shape[2]
    tm = _row_tile(m, (1040, 1024, 512, 256, 128, 64, 8))
    tn = _pick(n, (1024, 1152, 896, 640, 512, 384, 256, 128))
    tk = _pick(kd, (2048, 1024, 512))
    nk = kd // tk
    scratch = [pltpu.VMEM((tm, tn), F32)] if nk > 1 else []
    return pl.pallas_call(
        functools.partial(_mm_kernel, nk=nk, act=act),
        grid=(m // tm, n // tn, nk),
        in_specs=[pl.BlockSpec((tm, tk), lambda i, j, k: (i, k)),
                  pl.BlockSpec((1, tk, tn), lambda i, j, k: (layer, k, j))],
        out_specs=pl.BlockSpec((tm, tn), lambda i, j, k: (i, j)),
        out_shape=jax.ShapeDtypeStruct((m, n), out_dtype),
        scratch_shapes=scratch,
        compiler_params=_cparams(("parallel", "parallel", "arbitrary")),
        name=name,
    )(a, w)


def _mm_resnorm(a, w, layer, res, g_post, g_next, name="mm_resnorm"):
    m, kd = a.shape
    n = w.shape[2]
    tm = _row_tile(m, (520, 512, 256, 128, 64, 8))
    tk = _pick(kd, (2048, 1024, 512))
    res_spec = pl.BlockSpec((tm, n), lambda i, k: (i, 0))
    nk = kd // tk
    scratch = [pltpu.VMEM((tm, n), F32)] if nk > 1 else []
    return pl.pallas_call(
        functools.partial(_mm_resnorm_kernel, nk=nk),
        grid=(m // tm, nk),
        in_specs=[pl.BlockSpec((tm, tk), lambda i, k: (i, k)),
                  pl.BlockSpec((1, tk, n), lambda i, k: (layer, k, 0)),
                  res_spec,
                  pl.BlockSpec((1, n), lambda i, k: (0, 0)),
                  pl.BlockSpec((1, n), lambda i, k: (0, 0))],
        out_specs=[pl.BlockSpec((tm, n), lambda i, k: (i, 0)),
                   pl.BlockSpec((tm, n), lambda i, k: (i, 0))],
        out_shape=[jax.ShapeDtypeStruct((m, n), F32), jax.ShapeDtypeStruct((m, n), BF16)],
        scratch_shapes=scratch,
        compiler_params=_cparams(("parallel", "arbitrary")),
        name=name,
    )(a, w, res, g_post.reshape(1, n), g_next.reshape(1, n))


def _merge_kernel(y_ref, w_ref, g_ref, o_ref, acc_ref):
    n = pl.program_id(2)
    z = _dg(y_ref[0], w_ref[0, 0]) * g_ref[...].astype(F32)

    @pl.when(n == 0)
    def _():
        acc_ref[...] = z

    @pl.when(n > 0)
    def _():
        acc_ref[...] += z

    @pl.when(n == N_BRANCH - 1)
    def _():
        o_ref[...] = acc_ref[...].astype(o_ref.dtype)


def _merge(y3, wb, layer, gate):
    _, m, bw = y3.shape
    d = wb.shape[3]
    tm = _row_tile(m, (1040, 1024, 512, 256, 128, 64, 8))
    tn = 1024
    nj = d // tn
    return pl.pallas_call(
        _merge_kernel,
        grid=(m // tm, nj, N_BRANCH),
        in_specs=[pl.BlockSpec((1, tm, bw), lambda i, j, n: (n, i, 0)),
                  pl.BlockSpec((1, 1, bw, tn), lambda i, j, n: (layer, n, 0, j)),
                  pl.BlockSpec((tm, tn), lambda i, j, n: (i, n * nj + j))],
        out_specs=pl.BlockSpec((tm, tn), lambda i, j, n: (i, j)),
        out_shape=jax.ShapeDtypeStruct((m, d), BF16),
        scratch_shapes=[pltpu.VMEM((tm, tn), F32)],
        compiler_params=_cparams(("parallel", "parallel", "arbitrary")),
        name="merge",
    )(y3, wb, gate)


def _cast_kernel(w_ref, o_ref):
    o_ref[...] = w_ref[...].astype(o_ref.dtype)


def _cast_bf16(w):
    g, r, c = w.shape
    tr = _pick(r, (256, 128, 64, 8))
    return pl.pallas_call(
        _cast_kernel,
        grid=(g, r // tr),
        in_specs=[pl.BlockSpec((1, tr, c), lambda a, i: (a, i, 0))],
        out_specs=pl.BlockSpec((1, tr, c), lambda a, i: (a, i, 0)),
        out_shape=jax.ShapeDtypeStruct((g, r, c), BF16),
        compiler_params=_cparams(("parallel", "parallel")),
        name="cast_bf16",
    )(w)


_SEGS = ((0, RW_PROJ, RW_PAD),
         (RW_PROJ, SSD_PROJ, SSD_PAD),
         (RW_PROJ + SSD_PROJ, GLA_PROJ, GLA_PAD),
         (RW_PROJ + SSD_PROJ + GLA_PROJ, N_BRANCH * D_MODEL, N_BRANCH * D_MODEL))


def _split_t_kernel(w_ref, o_ref, *, live, tc):
    x = jnp.transpose(w_ref[0])
    col = pl.program_id(1) * tc + lax.broadcasted_iota(jnp.int32, x.shape, 1)
    o_ref[0] = jnp.where(col < live, x, 0.0).astype(o_ref.dtype)


def _split_w_in_t(w_in_t):
    depth, _, d = w_in_t.shape
    outs = []
    for first, live, padded in _SEGS:
        tc = _pick(padded, (1152, 896, 640, 1024))
        outs.append(pl.pallas_call(
            functools.partial(_split_t_kernel, live=live, tc=tc),
            grid=(depth, padded // tc),
            in_specs=[pl.BlockSpec((pl.Element(1), pl.Element(tc), pl.Element(d)),
                                   lambda l, j, f=first, t=tc: (l, pl.multiple_of(f + j * t, 16), 0))],
            out_specs=pl.BlockSpec((1, d, tc), lambda l, j: (l, 0, j)),
            out_shape=jax.ShapeDtypeStruct((depth, d, padded), BF16),
            compiler_params=_cparams(("parallel", "parallel")),
            name="split_w_in_t",
        )(w_in_t))
    return outs


def _attn_p_kernel(q_ref, k_ref, v_ref, o_ref):
    s = _dgb(q_ref[...], k_ref[0], NT) * (X_HEAD ** -0.5)
    p = jnp.exp(s - jnp.max(s, axis=-1, keepdims=True))
    attn = p / jnp.sum(p, axis=-1, keepdims=True)
    o_ref[...] = _dgb(attn, v_ref[0]).astype(o_ref.dtype)


def _attn_prompt(q, mem_k, mem_v, nb, t):
    tq = _pick(t, (1024, 512, 256, 128, 64))
    nq = t // tq
    return pl.pallas_call(
        _attn_p_kernel,
        grid=(nb, X_HEADS, nq),
        in_specs=[pl.BlockSpec((tq, X_HEAD), lambda b, h, i: (b * nq + i, h)),
                  pl.BlockSpec((1, N_MEM, X_HEAD), lambda b, h, i: (b, 0, h)),
                  pl.BlockSpec((1, N_MEM, X_HEAD), lambda b, h, i: (b, 0, h))],
        out_specs=pl.BlockSpec((tq, X_HEAD), lambda b, h, i: (b * nq + i, h)),
        out_shape=jax.ShapeDtypeStruct(q.shape, BF16),
        compiler_params=_cparams(("parallel", "parallel", "parallel")),
        name="attn_prompt",
    )(q, mem_k, mem_v)


def _attn_s_kernel(q_ref, k_ref, v_ref, o_ref):
    q = q_ref[0].astype(F32)
    s = jnp.sum(k_ref[0, 0] * q[None], axis=-1, keepdims=True) * (X_HEAD ** -0.5)
    p = jnp.exp(s - jnp.max(s, axis=0, keepdims=True))
    attn = p / jnp.sum(p, axis=0, keepdims=True)
    o_ref[0] = jnp.sum(attn * v_ref[0, 0], axis=0).astype(o_ref.dtype)


def _attn_sample(q3, cache_k, cache_v, layer):
    nb = q3.shape[0]
    kv_spec = pl.BlockSpec((1, 1, N_MEM, X_HEADS, X_HEAD), lambda b: (layer, b, 0, 0, 0))
    return pl.pallas_call(
        _attn_s_kernel,
        grid=(nb,),
        in_specs=[pl.BlockSpec((1, X_HEADS, X_HEAD), lambda b: (b, 0, 0)), kv_spec, kv_spec],
        out_specs=pl.BlockSpec((1, X_HEADS, X_HEAD), lambda b: (b, 0, 0)),
        out_shape=jax.ShapeDtypeStruct((nb, X_HEADS, X_HEAD), BF16),
        compiler_params=_cparams(("parallel",)),
        name="attn_sample",
    )(q3, cache_k, cache_v)


def _rwkv_tokens(p, prev, mu, w0, w2, a0, a2, g2, k_k, k_a):
    c = BRANCH_W
    ps = p + (prev - p) * mu
    r = ps[:, 0:c]
    k = ps[:, c:2 * c]
    v = ps[:, 2 * c:3 * c]
    slab = ps[:, 3 * c:3 * c + 128]
    gslab = ps[:, 3 * c + 128:RW_PAD]
    wl = w0 + _dgb(jnp.tanh(slab), w2)
    lw = -jnp.exp(-_softplus(-wl) - 0.5)
    a = _sigmoid(a0 + _dgb(slab, a2))
    g = _dgb(_sigmoid(gslab), g2)
    kkraw = k * k_k
    k2 = k * (1.0 + (a - 1.0) * k_a)
    return r, k2, v, lw, a, g, kkraw


def _rwkv_p_kernel(p_ref, mu_ref, w0_ref, w2_ref, a0_ref, a2_ref, g2_ref, kk_ref, ka_ref, rk_ref,
                   lnw_ref, lnb_ref, y_ref, s_ref, prev_scr):
    cidx = pl.program_id(1)

    @pl.when(cidx == 0)
    def _():
        s_ref[...] = jnp.zeros_like(s_ref)
        prev_scr[...] = jnp.zeros_like(prev_scr)

    p = p_ref[...]
    n = p.shape[0]
    row1 = lax.broadcasted_iota(jnp.int32, (n, 1), 0)
    prev = jnp.where(row1 == 0, prev_scr[0:1, :], pltpu.roll(p, 1, 0))
    prev_scr[0:1, :] = p[n - 1:n, :]
    r, k2, v, lw, a, g, kkraw = _rwkv_tokens(p, prev, mu_ref[...], w0_ref[...], w2_ref[...], a0_ref[...],
                                             a2_ref[...], g2_ref[...], kk_ref[...], ka_ref[...])
    cum = _dot_sel_l(_tri_incl(n), lw)
    e_c = jnp.exp(cum)
    e_x = jnp.exp(cum - lw)
    e_n = jnp.exp(-cum)
    e_l = jnp.exp(cum[n - 1:n, :] - cum)
    rk = rk_ref[...]
    lnw = lnw_ref[...]
    lnb = lnb_ref[...]

    m0, row, col = _pair_masks(n)
    strict = col < row
    incl = col <= row
    eye = jnp.where(col == row, 1.0, 0.0)
    ms, _, _ = _pair_masks(RW_HEAD)

    prs = range(RW_HEADS // 2)
    sls = [slice(pi * LANES, (pi + 1) * LANES) for pi in prs]
    kkn = []
    for sl in sls:
        kkp = kkraw[:, sl]
        kkn.append(kkp / jnp.maximum(jnp.sqrt(_headsum_pair(kkp * kkp, m0)), 1e-12))
    bv = [kkn[i] * a[:, sls[i]] for i in prs]
    at = [-kkn[i] * e_x[:, sls[i]] for i in prs]
    rt = [r[:, sl] * e_c[:, sl] for sl in sls]
    vv = [v[:, sl] for sl in sls]
    ar = [jnp.concatenate([at[i], rt[i]], axis=0) for i in prs]
    sab = [_dot_hi(ar[i], _bd(bv[i] * e_n[:, sls[i]], m0), NT) for i in prs]
    sak = [_dgb(ar[i], _bd(k2[:, sls[i]] * e_n[:, sls[i]], m0), NT) for i in prs]
    a_ab = [jnp.where(strict, s[:n], 0.0) for s in sab]
    a_ak = [jnp.where(strict, s[:n], 0.0) for s in sak]
    m_rb = [jnp.where(incl, s[n:], 0.0) for s in sab]
    m_rk = [jnp.where(incl, s[n:], 0.0) for s in sak]
    tinv = [eye + x for x in a_ab]
    xs = a_ab
    akv = [_dgb(a_ak[i], _bd(vv[i], m0)) for i in prs]
    for _ in range(n.bit_length() - 2):
        xs = [_dot_hi(x, _bd(x, m0)) for x in xs]
        tinv = [tinv[i] + _dot_hi(tinv[i], _bd(xs[i], m0)) for i in prs]
    tw = [_dgb(tinv[i], jnp.concatenate([_bd(at[i], m0), _bd(akv[i], m0)], axis=1)) for i in prs]
    s0 = [s_ref[0, pi] for pi in prs]
    us = [_dgb(jnp.concatenate([tw[i][:, :LANES], rt[i]], axis=0), _bd(s0[i], ms), NT) for i in prs]
    u = [us[i][:n] + tw[i][:, LANES:] for i in prs]
    uv = [jnp.concatenate([u[i], vv[i]], axis=0) for i in prs]
    y = [us[i][n:] + _dgb(jnp.concatenate([m_rb[i], m_rk[i]], axis=1),
                             jnp.concatenate([_bd(u[i], m0), _bd(vv[i], m0)], axis=0)) for i in prs]
    for i in prs:
        sl = sls[i]
        bk = jnp.concatenate([bv[i] * e_l[:, sl], k2[:, sl] * e_l[:, sl]], axis=0)
        z = _dgb(uv[i], bk, TN)
        s_ref[0, i] = s0[i] * e_c[n - 1:n, sl] + jnp.where(ms, z[:RW_HEAD], z[RW_HEAD:])
    for i in prs:
        sl = sls[i]
        mean = _headsum_pair(y[i], m0) * (1.0 / RW_HEAD)
        d = y[i] - mean
        var = _headsum_pair(d * d, m0) * (1.0 / RW_HEAD)
        yn = d * lax.rsqrt(var + RW_GN_EPS) * lnw[:, sl] + lnb[:, sl]
        bonus = _headsum_pair(r[:, sl] * k2[:, sl] * rk[:, sl], m0) * vv[i]
        y_ref[0, :, sl] = ((yn + bonus) * g[:, sl]).astype(y_ref.dtype)


def _rwkv_weights(w2, a2, g2):
    lw = w2.shape[0]
    w2p = jnp.concatenate([w2, jnp.zeros((LANES - lw, BRANCH_W), F32)], axis=0).astype(BF16)
    a2p = jnp.concatenate([jnp.zeros((lw, BRANCH_W), F32), a2], axis=0).astype(BF16)
    g2p = jnp.pad(g2, ((0, RW_PAD - 3 * BRANCH_W - LANES - g2.shape[0]), (0, 0))).astype(BF16)
    return w2p, a2p, g2p


def _row(x):
    return x.reshape(1, -1)


def _const_spec(shape):
    nd = len(shape)
    return pl.BlockSpec(shape, lambda *_: (0,) * nd)


def _chained_call(kernel_fn, prev, inputs, in_specs, **kw):
    if prev is None:
        return pl.pallas_call(kernel_fn, in_specs=in_specs, **kw)(*inputs)
    n_in = len(inputs)

    def body(*refs):
        return kernel_fn(*refs[:n_in], *refs[n_in + 1:])

    return pl.pallas_call(body, in_specs=list(in_specs) + [pl.BlockSpec(memory_space=pl.ANY)],
                          input_output_aliases={n_in: 0}, **kw)(*inputs, prev)


def _put_kernel(rows_ref, o_ref):
    o_ref[...] = rows_ref[...].astype(o_ref.dtype)


def _put_rows(buf, rows, blk):
    g, n, c = rows.shape
    return _chained_call(
        _put_kernel, buf, (rows,), [_const_spec(rows.shape)],
        grid=(1,),
        out_specs=pl.BlockSpec((g, n, c), lambda i: (0, blk, 0)),
        out_shape=jax.ShapeDtypeStruct(buf.shape, buf.dtype),
        compiler_params=_cparams(("arbitrary",)),
        name="put_rows",
    )


def _rwkv_prompt(proj, nb, t, wts, y3, rows):
    nc = t // CHUNK
    consts = wts
    in_specs = [pl.BlockSpec((CHUNK, RW_PAD), lambda b, c: (b * nc + c, 0))]
    in_specs += [_const_spec(x.shape) for x in consts]
    return _chained_call(
        _rwkv_p_kernel, y3, (proj, *consts), in_specs,
        grid=(nb, nc),
        out_specs=[pl.BlockSpec((1, CHUNK, BRANCH_W), lambda b, c: (0, b * nc + c, 0)),
                   pl.BlockSpec((1, RW_HEADS // 2, RW_HEAD, LANES), lambda b, c: (b, 0, 0, 0))],
        out_shape=[jax.ShapeDtypeStruct((N_BRANCH, rows, BRANCH_W), BF16),
                   jax.ShapeDtypeStruct((nb, RW_HEADS // 2, RW_HEAD, LANES), F32)],
        scratch_shapes=[pltpu.VMEM((8, RW_PAD), F32)],
        compiler_params=_cparams(("parallel", "arbitrary")),
        name="rwkv_prompt",
    )


def _unpack_rwkv_state(sp):
    nb = sp.shape[0]
    s = sp.reshape(nb, RW_HEADS // 2, RW_HEAD, 2, RW_HEAD)
    return jnp.transpose(s, (0, 1, 3, 2, 4)).reshape(nb, RW_HEADS, RW_HEAD, RW_HEAD)


def _rwkv_s_prep_kernel(p_ref, prev_ref, mu_ref, w0_ref, w2_ref, a0_ref, a2_ref, g2_ref, kk_ref, ka_ref, rk_ref,
                        o_ref):
    r, k2, v, lw, a, g, kkraw = _rwkv_tokens(p_ref[...], prev_ref[...], mu_ref[...], w0_ref[...], w2_ref[...],
                                             a0_ref[...], a2_ref[...], g2_ref[...], kk_ref[...], ka_ref[...])
    n = r.shape[0]
    m0, _, _ = _pair_masks(n)
    rk = rk_ref[...]
    o_ref[0] = r
    o_ref[1] = jnp.exp(lw)
    o_ref[2] = k2
    o_ref[3] = v
    o_ref[6] = g
    for pi in range(RW_HEADS // 2):
        sl = slice(pi * LANES, (pi + 1) * LANES)
        kkp = kkraw[:, sl]
        kkn = kkp / jnp.maximum(jnp.sqrt(_headsum_pair(kkp * kkp, m0)), 1e-12)
        o_ref[4, :, sl] = -kkn
        o_ref[5, :, sl] = kkn * a[:, sl]
        o_ref[7, :, sl] = _headsum_pair(r[:, sl] * k2[:, sl] * rk[:, sl], m0) * v[:, sl]


def _rwkv_s_kernel(vec_ref, s_ref, lnw_ref, lnb_ref, so_ref, y_ref, y_scr):
    r = vec_ref[0, 0]
    w = vec_ref[1, 0]
    k = vec_ref[2, 0]
    a = vec_ref[4, 0]
    b = vec_ref[5, 0]

    def body(i, carry):
        s = s_ref[0, 0, i]
        sa = jnp.sum(s * a, axis=0, keepdims=True)
        sn = s * w + sa * b + vec_ref[3, 0, pl.ds(i, 1), :] * k
        so_ref[0, 0, i] = sn
        y_scr[pl.ds(i, 1), :] = jnp.sum(sn * r, axis=0, keepdims=True)
        return carry

    lax.fori_loop(0, RW_HEAD, body, 0)
    y = y_scr[...]
    d = y - jnp.mean(y, axis=0, keepdims=True)
    var = jnp.mean(d * d, axis=0, keepdims=True)
    yn = d * lax.rsqrt(var + RW_GN_EPS) * lnw_ref[0] + lnb_ref[0]
    y_ref[0] = (yn + vec_ref[7, 0]) * vec_ref[6, 0]


def _rwkv_sample(proj, row0_blk, shift_prev, state_t, layer, wts, lnw, lnb, s_all):
    nb = shift_prev.shape[0]
    consts = wts
    vec = pl.pallas_call(
        _rwkv_s_prep_kernel,
        grid=(1,),
        in_specs=[pl.BlockSpec((nb, RW_PAD), lambda i: (row0_blk, 0)),
                  pl.BlockSpec((nb, RW_PAD), lambda i: (0, 0))] + [_const_spec(x.shape) for x in consts],
        out_specs=pl.BlockSpec((8, nb, BRANCH_W), lambda i: (0, 0, 0)),
        out_shape=jax.ShapeDtypeStruct((8, nb, BRANCH_W), F32),
        compiler_params=_cparams(("arbitrary",)),
        name="rwkv_sample_prep",
    )(proj, shift_prev, *consts)
    vec_t = jnp.transpose(vec, (0, 2, 1)).reshape(8, RW_HEADS, RW_HEAD, nb)
    ln_shape = (RW_HEADS, RW_HEAD, nb)
    lnw_t = jnp.broadcast_to(lnw.reshape(RW_HEADS, RW_HEAD, 1), ln_shape)
    lnb_t = jnp.broadcast_to(lnb.reshape(RW_HEADS, RW_HEAD, 1), ln_shape)
    st_spec = pl.BlockSpec((1, 1, RW_HEAD, RW_HEAD, nb), lambda h: (layer, h, 0, 0, 0))
    ch_spec = pl.BlockSpec((1, RW_HEAD, nb), lambda h: (h, 0, 0))
    s_all, y_t = _chained_call(
        _rwkv_s_kernel, s_all, (vec_t, state_t, lnw_t, lnb_t),
        [pl.BlockSpec((8, 1, RW_HEAD, nb), lambda h: (0, h, 0, 0)), st_spec, ch_spec, ch_spec],
        grid=(RW_HEADS,),
        out_specs=[st_spec, ch_spec],
        out_shape=[jax.ShapeDtypeStruct(state_t.shape, F32), jax.ShapeDtypeStruct(ln_shape, F32)],
        scratch_shapes=[pltpu.VMEM((RW_HEAD, nb), F32)],
        compiler_params=_cparams(("parallel",)),
        name="rwkv_sample",
    )
    return jnp.transpose(y_t.reshape(BRANCH_W, nb)), s_all


def _expand_heads():
    k = lax.broadcasted_iota(jnp.int32, (LANES, BRANCH_W), 0)
    c = lax.broadcasted_iota(jnp.int32, (LANES, BRANCH_W), 1)
    return jnp.where(jnp.right_shift(c, 6) == k, 1.0, 0.0).astype(BF16)


def _ssd_p_kernel(p_ref, cw_ref, cb_ref, dtb_ref, alog_ref, dsk_ref, nw_ref, y_ref, st_ref, buf_scr):
    cidx = pl.program_id(1)
    n = p_ref.shape[0]

    @pl.when(cidx == 0)
    def _():
        st_ref[...] = jnp.zeros_like(st_ref)
        buf_scr[0:8, :] = jnp.zeros((8, SSD_CONV_DIM), F32)

    z = p_ref[:, 0:BRANCH_W]
    buf_scr[8:8 + n, :] = p_ref[:, BRANCH_W:BRANCH_W + SSD_CONV_DIM]
    conv = cb_ref[...]
    for i in range(SSD_CONV):
        conv = conv + cw_ref[i:i + 1, :] * buf_scr[pl.ds(8 - (SSD_CONV - 1) + i, n), :]
    buf_scr[0:8, :] = buf_scr[n:n + 8, :]
    xa = _silu(conv)
    xs = xa[:, 0:BRANCH_W]
    bm = xa[:, BRANCH_W:BRANCH_W + SSD_GROUPS * SSD_STATE]
    cm = xa[:, BRANCH_W + SSD_GROUPS * SSD_STATE:]
    dt = _softplus(p_ref[:, BRANCH_W + SSD_CONV_DIM:SSD_PAD] + dtb_ref[...])
    dte = _dot_sel_r(dt, _expand_heads())
    da = dte * (-jnp.exp(alog_ref[...]))
    cum = _dot_sel_l(_tri_incl(n), da)
    xdt = xs * dte
    ecum = jnp.exp(cum)
    cl = cum[n - 1:n, :]
    xdl = xdt * jnp.exp(cl - cum)
    pl_ = jnp.exp(cl)

    m0, row, col = _pair_masks(n)
    incl = col <= row
    eye = col == row
    gw = BRANCH_W // SSD_GROUPS
    ppg = gw // LANES
    ys = []
    for gi in range(SSD_GROUPS):
        gs = slice(gi * gw, (gi + 1) * gw)
        bg = bm[:, gi * SSD_STATE:(gi + 1) * SSD_STATE]
        cg = cm[:, gi * SSD_STATE:(gi + 1) * SSD_STATE]
        cbp = _dgb(cg, jnp.concatenate([bg, bg], axis=0), NT)
        st = st_ref[0, :, gs]
        cs = _dgb(cg, st)
        for q in range(ppg):
            sl = slice(gi * gw + q * LANES, gi * gw + (q + 1) * LANES)
            cp = cum[:, sl]
            rp = jnp.sum(jnp.where(eye, cp, 0.0), axis=0, keepdims=True)
            seg = jnp.exp(jnp.where(incl, cp - rp, -jnp.inf))
            yp = _dgb(cbp * seg, _bd(xdt[:, sl], m0)) + ecum[:, sl] * cs[:, q * LANES:(q + 1) * LANES]
            ys.append(yp)
        st_ref[0, :, gs] = st * pl_[:, gs] + _dgb(bg, xdl[:, gs], TN)
    y = jnp.concatenate(ys, axis=1) + xs * dsk_ref[...]
    y = y * _silu(z)
    for gi in range(SSD_GROUPS):
        gs = slice(gi * gw, (gi + 1) * gw)
        yg = y[:, gs]
        yg = yg * lax.rsqrt(jnp.mean(yg * yg, axis=-1, keepdims=True) + GROUP_EPS)
        y_ref[0, :, gs] = (yg * nw_ref[:, gs]).astype(y_ref.dtype)


def _ssd_consts(conv_w, conv_b, dt_bias, a_log, d_skip, norm_w):
    dtb = jnp.pad(dt_bias, (0, LANES - SSD_HEADS)).reshape(1, LANES)
    return (conv_w, _row(conv_b), dtb, _row(jnp.repeat(a_log, SSD_HEAD)), _row(jnp.repeat(d_skip, SSD_HEAD)),
            _row(norm_w))


def _ssd_prompt(proj, nb, t, consts, y3):
    nc = t // CHUNK
    return _chained_call(
        _ssd_p_kernel, y3, (proj, *consts),
        [pl.BlockSpec((CHUNK, SSD_PAD), lambda b, c: (b * nc + c, 0))] + [_const_spec(x.shape) for x in consts],
        grid=(nb, nc),
        out_specs=[pl.BlockSpec((1, CHUNK, BRANCH_W), lambda b, c: (1, b * nc + c, 0)),
                   pl.BlockSpec((1, SSD_STATE, BRANCH_W), lambda b, c: (b, 0, 0))],
        out_shape=[jax.ShapeDtypeStruct(y3.shape, BF16),
                   jax.ShapeDtypeStruct((nb, SSD_STATE, BRANCH_W), F32)],
        scratch_shapes=[pltpu.VMEM((CHUNK + 8, SSD_CONV_DIM), F32)],
        compiler_params=_cparams(("parallel", "arbitrary")),
        name="ssd_prompt",
    )


def _unpack_ssd_state(st):
    nb = st.shape[0]
    return jnp.transpose(st.reshape(nb, SSD_STATE, SSD_HEADS, SSD_HEAD), (0, 2, 3, 1))


def _ssd_s_prep_kernel(p_ref, cv_ref, cw_ref, cb_ref, dtb_ref, alog_ref, dsk_ref, o_ref, bc_ref):
    z = p_ref[:, 0:BRANCH_W]
    conv = cb_ref[...] + cw_ref[SSD_CONV - 1:SSD_CONV, :] * p_ref[:, BRANCH_W:BRANCH_W + SSD_CONV_DIM]
    for i in range(SSD_CONV - 1):
        conv = conv + cw_ref[i:i + 1, :] * cv_ref[i]
    xa = _silu(conv)
    xs = xa[:, 0:BRANCH_W]
    dt = _softplus(p_ref[:, BRANCH_W + SSD_CONV_DIM:SSD_PAD] + dtb_ref[...])
    dte = _dot_sel_r(dt, _expand_heads())
    o_ref[0] = xs * dte
    o_ref[1] = jnp.exp(dte * (-jnp.exp(alog_ref[...])))
    o_ref[2] = xs * dsk_ref[...]
    o_ref[3] = _silu(z)
    bc_ref[...] = xa[:, BRANCH_W:]


def _ssd_s_kernel(vec_ref, bc_ref, s_ref, nw_ref, so_ref, y_ref):
    nbb = s_ref.shape[1]
    n = SSD_HEAD
    eye = lax.broadcasted_iota(jnp.int32, (n, n), 0) == lax.broadcasted_iota(jnp.int32, (n, n), 1)
    hpg = SSD_HEADS // SSD_GROUPS

    def body(bi, carry):
        for h in range(SSD_HEADS):
            hs = pl.ds(h, 1)
            gi = h // hpg
            s = s_ref[0, bi, h]
            xdt_ = vec_ref[0, bi, hs, :]
            dec = vec_ref[1, bi, hs, 0:1]
            b_ = bc_ref[bi, pl.ds(gi, 1), :]
            c_ = bc_ref[bi, pl.ds(SSD_GROUPS + gi, 1), :]
            xc = jnp.sum(jnp.where(eye, xdt_, 0.0), axis=-1, keepdims=True)
            sn = s * dec + xc * b_
            so_ref[0, bi, h] = sn
            yc = jnp.sum(sn * c_, axis=-1, keepdims=True)
            yr = jnp.sum(jnp.where(eye, yc, 0.0), axis=0, keepdims=True)
            y_ref[bi, hs, :] = (yr + vec_ref[2, bi, hs, :]) * vec_ref[3, bi, hs, :]
        for gi in range(SSD_GROUPS):
            rs = pl.ds(gi * hpg, hpg)
            yg = y_ref[bi, rs, :]
            ms = jnp.sum(jnp.sum(yg * yg, axis=-1, keepdims=True), axis=0, keepdims=True) * (1.0 / (hpg * n))
            y_ref[bi, rs, :] = yg * lax.rsqrt(ms + GROUP_EPS) * nw_ref[rs, :]
        return carry

    lax.fori_loop(0, nbb, body, 0)


def _ssd_sample(proj, row0_blk, conv_prev, state, layer, consts, s_all):
    nb = conv_prev.shape[1]
    cw, cb, dtb, alog, dsk, nw = consts
    vec, bc = pl.pallas_call(
        _ssd_s_prep_kernel,
        grid=(1,),
        in_specs=[pl.BlockSpec((nb, SSD_PAD), lambda i: (row0_blk, 0)),
                  _const_spec(conv_prev.shape)] + [_const_spec(x.shape) for x in (cw, cb, dtb, alog, dsk)],
        out_specs=[_const_spec((4, nb, BRANCH_W)), _const_spec((nb, 2 * SSD_GROUPS * SSD_STATE))],
        out_shape=[jax.ShapeDtypeStruct((4, nb, BRANCH_W), F32),
                   jax.ShapeDtypeStruct((nb, 2 * SSD_GROUPS * SSD_STATE), F32)],
        compiler_params=_cparams(("arbitrary",)),
        name="ssd_sample_prep",
    )(proj, conv_prev, cw, cb, dtb, alog, dsk)
    vec = vec.reshape(4, nb, SSD_HEADS, SSD_HEAD)
    bc = bc.reshape(nb, 2 * SSD_GROUPS, SSD_STATE)
    bb = 8
    st_spec = pl.BlockSpec((1, bb, SSD_HEADS, SSD_HEAD, SSD_STATE), lambda i: (layer, i, 0, 0, 0))
    s_all, y = _chained_call(
        _ssd_s_kernel, s_all, (vec, bc, state, nw.reshape(SSD_HEADS, SSD_HEAD)),
        [pl.BlockSpec((4, bb, SSD_HEADS, SSD_HEAD), lambda i: (0, i, 0, 0)),
         pl.BlockSpec((bb, 2 * SSD_GROUPS, SSD_STATE), lambda i: (i, 0, 0)), st_spec,
         _const_spec((SSD_HEADS, SSD_HEAD))],
        grid=(nb // bb,),
        out_specs=[st_spec, pl.BlockSpec((bb, SSD_HEADS, SSD_HEAD), lambda i: (i, 0, 0))],
        out_shape=[jax.ShapeDtypeStruct(state.shape, F32),
                   jax.ShapeDtypeStruct((nb, SSD_HEADS, SSD_HEAD), F32)],
        compiler_params=_cparams(("parallel",)),
        name="ssd_sample",
    )
    return y.reshape(nb, BRANCH_W), s_all


def _gla_tokens(p, au, ab):
    q = p[:, 0:GLA_DK] * (GLA_HK ** -0.5)
    k = p[:, GLA_DK:2 * GLA_DK]
    v = p[:, 2 * GLA_DK:2 * GLA_DK + GLA_DV]
    r = p[:, 2 * GLA_DK + GLA_DV:2 * GLA_DK + 2 * GLA_DV]
    ad = p[:, 2 * GLA_DK + 2 * GLA_DV:GLA_PAD]
    lg = -_softplus(-(_dgb(ad, au) + ab)) * (1.0 / GLA_TAU)
    return q, k, v, r, lg


def _gla_p_kernel(p_ref, au_ref, ab_ref, nw_ref, o_ref, st_ref):
    cidx = pl.program_id(1)

    @pl.when(cidx == 0)
    def _():
        st_ref[...] = jnp.zeros_like(st_ref)

    n = p_ref.shape[0]
    q, k, v, r, lg = _gla_tokens(p_ref[...], au_ref[...], ab_ref[...])
    cum = _dot_sel_l(_tri_incl(n), lg)
    cumx = cum - lg
    cl = cum[n - 1:n, :]
    qe = q * jnp.exp(cum)
    kb = k * jnp.exp(cl - cum)
    pl_ = jnp.exp(cl)
    rowi = lax.broadcasted_iota(jnp.int32, (SUB, 1), 0)
    for h in range(GLA_HEADS):
        sk = slice(h * GLA_HK, (h + 1) * GLA_HK)
        sv = slice(h * GLA_HV, (h + 1) * GLA_HV)
        st = st_ref[0, h]
        vh = v[:, sv]
        parts = []
        for blk in range(n // SUB):
            lo = blk * SUB
            rs = slice(lo, lo + SUB)
            q_i = q[rs, sk]
            c_i = cum[rs, sk]
            k_i = k[rs, sk]
            v_i = vh[rs]
            if blk > 0:
                cref = cumx[lo:lo + 1, sk]
                qt = q_i * jnp.exp(c_i - cref)
                kt = k[0:lo, sk] * jnp.exp(cref - cum[0:lo, sk])
                acc = _dgb(_dgb(qt, kt, NT), vh[0:lo])
            else:
                acc = jnp.zeros((SUB, GLA_HV), F32)
            for s in range(SUB):
                e = jnp.exp(c_i - c_i[s:s + 1])
                w = jnp.sum(q_i * k_i[s:s + 1] * e, axis=-1, keepdims=True)
                acc = acc + jnp.where(rowi >= s, w, 0.0) * v_i[s:s + 1]
            parts.append(acc)
        o = jnp.concatenate(parts, axis=0) + _dgb(qe[:, sk], st, NT)
        st_ref[0, h] = st * pl_[:, sk] + _dgb(vh, kb[:, sk], TN)
        o = o * lax.rsqrt(jnp.mean(o * o, axis=-1, keepdims=True) + GROUP_EPS) * nw_ref[...]
        o_ref[0, :, sv] = (o * _silu(r[:, sv])).astype(o_ref.dtype)


def _gla_consts(alpha_up, alpha_b, norm_w):
    au = jnp.pad(alpha_up, ((0, LANES - GLA_LORA), (0, 0))).astype(BF16)
    return au, _row(alpha_b), _row(norm_w)


def _gla_prompt(proj, nb, t, consts, y3):
    nc = t // CHUNK
    return _chained_call(
        _gla_p_kernel, y3, (proj, *consts),
        [pl.BlockSpec((CHUNK, GLA_PAD), lambda b, c: (b * nc + c, 0))] + [_const_spec(x.shape) for x in consts],
        grid=(nb, nc),
        out_specs=[pl.BlockSpec((1, CHUNK, GLA_DV), lambda b, c: (2, b * nc + c, 0)),
                   pl.BlockSpec((1, GLA_HEADS, GLA_HV, GLA_HK), lambda b, c: (b, 0, 0, 0))],
        out_shape=[jax.ShapeDtypeStruct(y3.shape, BF16),
                   jax.ShapeDtypeStruct((nb, GLA_HEADS, GLA_HV, GLA_HK), F32)],
        compiler_params=_cparams(("parallel", "arbitrary")),
        name="gla_prompt",
    )


def _gla_s_prep_kernel(p_ref, au_ref, ab_ref, qk_ref, vr_ref):
    q, k, v, r, lg = _gla_tokens(p_ref[...], au_ref[...], ab_ref[...])
    qk_ref[0] = q
    qk_ref[1] = k
    qk_ref[2] = jnp.exp(lg)
    vr_ref[0] = v
    vr_ref[1] = _silu(r)


def _gla_s_kernel(qk_ref, vr_ref, s_ref, nw_ref, so_ref, y_ref):
    nbb = s_ref.shape[1]
    n = GLA_HK
    eye = lax.broadcasted_iota(jnp.int32, (n, n), 0) == lax.broadcasted_iota(jnp.int32, (n, n), 1)

    def col(x):
        return jnp.sum(jnp.where(eye, x, 0.0), axis=-1, keepdims=True)

    def body(bi, carry):
        for h in range(GLA_HEADS):
            hs = pl.ds(h, 1)
            s = s_ref[0, bi, h]
            sn = s * col(qk_ref[2, bi, hs, :]) + col(qk_ref[1, bi, hs, :]) * vr_ref[0, bi, hs, :]
            so_ref[0, bi, h] = sn
            o = jnp.sum(sn * col(qk_ref[0, bi, hs, :]), axis=0, keepdims=True)
            o = o * lax.rsqrt(jnp.mean(o * o, axis=-1, keepdims=True) + GROUP_EPS) * nw_ref[...]
            y_ref[bi, hs, :] = o * vr_ref[1, bi, hs, :]
        return carry

    lax.fori_loop(0, nbb, body, 0)


def _gla_sample(proj, row0_blk, nb, state, layer, consts, s_all):
    au, ab, nw = consts
    qk, vr = pl.pallas_call(
        _gla_s_prep_kernel,
        grid=(1,),
        in_specs=[pl.BlockSpec((nb, GLA_PAD), lambda i: (row0_blk, 0)), _const_spec(au.shape),
                  _const_spec(ab.shape)],
        out_specs=[_const_spec((3, nb, GLA_DK)), _const_spec((2, nb, GLA_DV))],
        out_shape=[jax.ShapeDtypeStruct((3, nb, GLA_DK), F32), jax.ShapeDtypeStruct((2, nb, GLA_DV), F32)],
        compiler_params=_cparams(("arbitrary",)),
        name="gla_sample_prep",
    )(proj, au, ab)
    qk = qk.reshape(3, nb, GLA_HEADS, GLA_HK)
    vr = vr.reshape(2, nb, GLA_HEADS, GLA_HV)
    bb = 8
    st_spec = pl.BlockSpec((1, bb, GLA_HEADS, GLA_HK, GLA_HV), lambda i: (layer, i, 0, 0, 0))
    s_all, y = _chained_call(
        _gla_s_kernel, s_all, (qk, vr, state, nw),
        [pl.BlockSpec((3, bb, GLA_HEADS, GLA_HK), lambda i: (0, i, 0, 0)),
         pl.BlockSpec((2, bb, GLA_HEADS, GLA_HV), lambda i: (0, i, 0, 0)), st_spec, _const_spec(nw.shape)],
        grid=(nb // bb,),
        out_specs=[st_spec, pl.BlockSpec((bb, GLA_HEADS, GLA_HV), lambda i: (i, 0, 0))],
        out_shape=[jax.ShapeDtypeStruct(state.shape, F32),
                   jax.ShapeDtypeStruct((nb, GLA_HEADS, GLA_HV), F32)],
        compiler_params=_cparams(("parallel",)),
        name="gla_sample",
    )
    return y.reshape(nb, GLA_DV), s_all


def kernel(x_prompt, x_sample, state_rwkv, state_rwkv_shift, state_ssd, state_ssd_conv, state_gla, cache_mem_k, cache_mem_v, mem_prompt, norm_mix_pre, norm_mix_post, norm_x_pre, norm_x_post, norm_ffn_pre, norm_ffn_post, w_in, rw_mu, rw_w0, rw_w2, rw_a0, rw_a2, rw_g2, rw_kk, rw_ka, rw_rk, rw_ln_w, rw_ln_b, ssd_conv_w, ssd_conv_b, ssd_dt_bias, ssd_a_log, ssd_d, ssd_norm_w, gla_alpha_up, gla_alpha_b, gla_norm_w, w_branch, w_out, x_mem_norm, x_wq, x_wk, x_wv, x_wo, ffn_up, ffn_down):
    nbp, t, d = x_prompt.shape
    nbs = x_sample.shape[0]
    depth = w_in.shape[0]
    mp = nbp * t
    m = mp + nbs
    sblk = mp // nbs

    x = jnp.concatenate([x_prompt.reshape(mp, d), x_sample.reshape(nbs, d)], axis=0)
    mem_rows = mem_prompt.reshape(nbp * N_MEM, d)
    tm_norm = _row_tile(m, (1040, 1024, 512, 256, 128, 64, 8))

    w_rw_all, w_ssd_all, w_gla_all, w_gate_all = _split_w_in_t(jnp.transpose(w_in, (0, 2, 1)))
    wb_all = _cast_bf16(w_branch.reshape(depth * N_BRANCH, BRANCH_W, d)).reshape(depth, N_BRANCH, BRANCH_W, d)
    w_out_b, wq_b, wk_b, wv_b, wo_b, up_b, down_b = [_cast_bf16(w) for w in
                                                     (w_out, x_wq, x_wk, x_wv, x_wo, ffn_up, ffn_down)]

    xn = _norm_rows(x, norm_mix_pre[0], tm_norm)
    p_acc = [[] for _ in range(7)]
    s_small = [[] for _ in range(2)]
    rw_ss = ssd_ss = gla_ss = None
    state_rwkv_t = jnp.transpose(state_rwkv, (0, 2, 3, 4, 1))
    for l in range(depth):
        proj_rw = _mm(xn, w_rw_all, l, name="proj_rwkv")
        proj_ssd = _mm(xn, w_ssd_all, l, name="proj_ssd")
        proj_gla = _mm(xn, w_gla_all, l, name="proj_gla")
        gate = _mm(xn, w_gate_all, l, out_dtype=BF16, act="sigmoid", name="proj_gate")

        w2p, a2p, g2p = _rwkv_weights(rw_w2[l], rw_a2[l], rw_g2[l])
        rw_tok = (_row(jnp.pad(rw_mu[l], (0, RW_PAD - RW_PROJ))), _row(rw_w0[l]), w2p, _row(rw_a0[l]), a2p, g2p,
                  _row(rw_kk[l]), _row(rw_ka[l]), _row(rw_rk[l]))
        y3, rw_sp = _rwkv_prompt(proj_rw, nbp, t, rw_tok + (_row(rw_ln_w[l]), _row(rw_ln_b[l])), None, m)
        shift_prev = jnp.pad(state_rwkv_shift[l], ((0, 0), (0, RW_PAD - RW_PROJ)))
        ya_s, rw_ss = _rwkv_sample(proj_rw, sblk, shift_prev, state_rwkv_t, l, rw_tok, rw_ln_w[l], rw_ln_b[l], rw_ss)

        ssd_c = _ssd_consts(ssd_conv_w[l], ssd_conv_b[l], ssd_dt_bias[l], ssd_a_log[l], ssd_d[l], ssd_norm_w[l])
        y3, ssd_sp = _ssd_prompt(proj_ssd, nbp, t, ssd_c, y3)
        conv_prev = jnp.transpose(state_ssd_conv[l], (1, 0, 2))
        yb_s, ssd_ss = _ssd_sample(proj_ssd, sblk, conv_prev, state_ssd, l, ssd_c, ssd_ss)

        gla_c = _gla_consts(gla_alpha_up[l], gla_alpha_b[l], gla_norm_w[l])
        y3, gla_sp = _gla_prompt(proj_gla, nbp, t, gla_c, y3)
        yc_s, gla_ss = _gla_sample(proj_gla, sblk, nbs, state_gla, l, gla_c, gla_ss)

        y3 = _put_rows(y3, jnp.stack([ya_s, yb_s, yc_s]), sblk)
        mix = _merge(y3, wb_all, l, gate)
        x, xn = _mm_resnorm(mix, w_out_b, l, x, norm_mix_post[l], norm_x_pre[l], name="out_proj")

        mn = _norm_rows(mem_rows, x_mem_norm[l], _row_tile(mem_rows.shape[0], (1024, 512, 256, 128, 64, 8)))
        mk = _mm(mn, wk_b, l, name="mem_k")
        mv = _mm(mn, wv_b, l, name="mem_v")
        q = _mm(xn, wq_b, l, out_dtype=BF16, name="attn_q")
        o = _attn_prompt(q, mk.reshape(nbp, N_MEM, d), mv.reshape(nbp, N_MEM, d), nbp, t)
        o_s = _attn_sample(q[mp:].reshape(nbs, X_HEADS, X_HEAD), cache_mem_k, cache_mem_v, l)
        o = _put_rows(o.reshape(1, m, d), o_s.reshape(1, nbs, d), sblk).reshape(m, d)
        x, xn = _mm_resnorm(o, wo_b, l, x, norm_x_post[l], norm_ffn_pre[l], name="attn_out")

        hf = _mm(xn, up_b, l, out_dtype=BF16, act="relu2", name="ffn_up")
        g_next = norm_mix_pre[l + 1] if l + 1 < depth else norm_mix_pre[l]
        x, xn = _mm_resnorm(hf, down_b, l, x, norm_ffn_post[l], g_next, name="ffn_down")

        last = [b * t + t - 1 for b in range(nbp)]
        rw_shift_p = jnp.stack([proj_rw[i, :RW_PROJ] for i in last])
        ssd_conv_p = jnp.stack([proj_ssd[i - (SSD_CONV - 2):i + 1, BRANCH_W:BRANCH_W + SSD_CONV_DIM] for i in last])
        new_p = (_unpack_rwkv_state(rw_sp), rw_shift_p, _unpack_ssd_state(ssd_sp), ssd_conv_p,
                 jnp.transpose(gla_sp, (0, 1, 3, 2)),
                 mk.reshape(nbp, N_MEM, X_HEADS, X_HEAD), mv.reshape(nbp, N_MEM, X_HEADS, X_HEAD))
        xbc_s = proj_ssd[mp:, BRANCH_W:BRANCH_W + SSD_CONV_DIM]
        new_s = (proj_rw[mp:, :RW_PROJ],
                 jnp.concatenate([state_ssd_conv[l][:, 1:], xbc_s[:, None, :]], axis=1))
        for acc, val in zip(p_acc, new_p):
            acc.append(val)
        for acc, val in zip(s_small, new_s):
            acc.append(val)

    outs_p = [jnp.stack(a) for a in p_acc]
    s_shift, s_conv = [jnp.stack(a) for a in s_small]
    return (x[:mp].reshape(nbp, t, d), x[mp:].reshape(nbs, 1, d), *outs_p,
            jnp.transpose(rw_ss, (0, 4, 1, 2, 3)), s_shift, ssd_ss, s_conv, gla_ss)
```

```python
import functools

import jax
import jax.numpy as jnp
from jax import lax
from jax.experimental import pallas as pl
from jax.experimental.pallas import tpu as pltpu

F32 = jnp.float32
BF16 = jnp.bfloat16

D_MODEL = 2048
BRANCH_W = 1024
N_BRANCH = 3
RW_HEADS = 16
RW_HEAD = 64
RW_PROJ = 3360
RW_PAD = 3456
RW_GN_EPS = 64e-5
SSD_HEADS = 16
SSD_HEAD = 64
SSD_GROUPS = 2
SSD_STATE = 128
SSD_CONV = 4
SSD_CONV_DIM = 1536
SSD_PROJ = 2576
SSD_PAD = 2688
GLA_HEADS = 4
GLA_DK = 512
GLA_DV = 1024
GLA_HK = 128
GLA_HV = 256
GLA_LORA = 16
GLA_TAU = 16.0
GLA_PROJ = 3088
GLA_PAD = 3200
N_MEM = 256
X_HEADS = 4
X_HEAD = 512
D_FF = 8192
NORM_EPS = 1e-6
GROUP_EPS = 1e-5
CHUNK = 64
SUB = 16
LANES = 128
VMEM_LIMIT = 56 * 1024 * 1024

NN = ((1,), (0,))
NT = ((1,), (1,))
TN = ((0,), (0,))


def _cparams(sem):
    return pltpu.CompilerParams(dimension_semantics=sem, vmem_limit_bytes=VMEM_LIMIT)


def _dg(a, b, dims=NN):
    return lax.dot_general(a, b, (dims, ((), ())), preferred_element_type=F32)


def _dgb(a, b, dims=NN):
    return _dg(a.astype(BF16), b.astype(BF16), dims)


def _hl(x):
    h = x.astype(BF16)
    return h, (x - h.astype(F32)).astype(BF16)


def _dot_hi(a, b, dims=NN):
    ah, al = _hl(a)
    bh, bl = _hl(b)
    return _dg(ah, bh, dims) + (_dg(ah, bl, dims) + _dg(al, bh, dims))


def _split3(x):
    h = x.astype(BF16)
    r = x - h.astype(F32)
    m = r.astype(BF16)
    return h, m, (r - m.astype(F32)).astype(BF16)


def _dot_sel_l(sel, x):
    h, m, l = _split3(x)
    return _dg(sel, h) + (_dg(sel, m) + _dg(sel, l))


def _dot_sel_r(x, sel):
    h, m, l = _split3(x)
    return _dg(h, sel) + (_dg(m, sel) + _dg(l, sel))


def _softplus(x):
    return jnp.maximum(x, 0.0) + jnp.log1p(jnp.exp(-jnp.abs(x)))


def _sigmoid(x):
    return 1.0 / (1.0 + jnp.exp(-x))


def _silu(x):
    return x * _sigmoid(x)


def _tri_incl(n):
    r = lax.broadcasted_iota(jnp.int32, (n, n), 0)
    c = lax.broadcasted_iota(jnp.int32, (n, n), 1)
    return jnp.where(c <= r, 1.0, 0.0).astype(BF16)


def _pair_masks(rows):
    lane = lax.broadcasted_iota(jnp.int32, (rows, LANES), 1)
    row = lax.broadcasted_iota(jnp.int32, (rows, LANES), 0)
    m0 = lane < 64
    col = jnp.bitwise_and(lane, 63)
    return m0, row, col


def _headsum_pair(x, m0):
    s0 = jnp.sum(jnp.where(m0, x, 0.0), axis=-1, keepdims=True)
    s1 = jnp.sum(jnp.where(m0, 0.0, x), axis=-1, keepdims=True)
    return jnp.where(m0, s0, s1)


def _bd(x, m0):
    return jnp.concatenate([jnp.where(m0, x, 0.0), jnp.where(m0, 0.0, x)], axis=0)


def _norm_kernel(x_ref, g_ref, o_ref):
    x = x_ref[...]
    y = x * lax.rsqrt(jnp.mean(x * x, axis=-1, keepdims=True) + NORM_EPS)
    o_ref[...] = (y * g_ref[...]).astype(o_ref.dtype)


def _norm_rows(x, g, tm):
    m, d = x.shape
    return pl.pallas_call(
        _norm_kernel,
        grid=(m // tm,),
        in_specs=[pl.BlockSpec((tm, d), lambda i: (i, 0)), pl.BlockSpec((1, d), lambda i: (0, 0))],
        out_specs=pl.BlockSpec((tm, d), lambda i: (i, 0)),
        out_shape=jax.ShapeDtypeStruct((m, d), BF16),
        compiler_params=_cparams(("parallel",)),
        name="rmsnorm",
    )(x, g.reshape(1, d))


def _mm_kernel(a_ref, w_ref, o_ref, *scratch, nk, act):
    part = _dg(a_ref[...], w_ref[0])

    def finish(acc):
        if act == "relu2":
            acc = jnp.square(jnp.maximum(acc, 0.0))
        elif act == "sigmoid":
            acc = _sigmoid(acc)
        o_ref[...] = acc.astype(o_ref.dtype)

    if nk == 1:
        finish(part)
        return
    acc_ref, = scratch
    k = pl.program_id(2)

    @pl.when(k == 0)
    def _():
        acc_ref[...] = part

    @pl.when(k > 0)
    def _():
        acc_ref[...] += part

    @pl.when(k == nk - 1)
    def _():
        finish(acc_ref[...])


def _mm_resnorm_kernel(a_ref, w_ref, res_ref, gp_ref, gn_ref, x_ref, xn_ref, *scratch, nk):
    part = _dg(a_ref[...], w_ref[0])

    def finish(acc):
        y = acc * lax.rsqrt(jnp.mean(acc * acc, axis=-1, keepdims=True) + NORM_EPS) * gp_ref[...]
        x = res_ref[...] + y
        x_ref[...] = x
        xn = x * lax.rsqrt(jnp.mean(x * x, axis=-1, keepdims=True) + NORM_EPS) * gn_ref[...]
        xn_ref[...] = xn.astype(xn_ref.dtype)

    if nk == 1:
        finish(part)
        return
    acc_ref, = scratch
    k = pl.program_id(1)

    @pl.when(k == 0)
    def _():
        acc_ref[...] = part

    @pl.when(k > 0)
    def _():
        acc_ref[...] += part

    @pl.when(k == nk - 1)
    def _():
        finish(acc_ref[...])


def _pick(n, cands):
    for c in cands:
        if n % c == 0:
            return c
    return n


def _row_tile(m, cands):
    return _pick(m, cands)


def _mm(a, w, layer, out_dtype=F32, act=None, name="mm"):
    m, kd = a.shape
    n = w.shape[2]
    tm = _row_tile(m, (1040, 1024, 512, 256, 128, 64, 8))
    tn = _pick(n, (1024, 1152, 896, 640, 512, 384, 256, 128))
    tk = _pick(kd, (2048, 1024, 512))
    nk = kd // tk
    scratch = [pltpu.VMEM((tm, tn), F32)] if nk > 1 else []
    return pl.pallas_call(
        functools.partial(_mm_kernel, nk=nk, act=act),
        grid=(m // tm, n // tn, nk),
        in_specs=[pl.BlockSpec((tm, tk), lambda i, j, k: (i, k)),
                  pl.BlockSpec((1, tk, tn), lambda i, j, k: (layer, k, j))],
        out_specs=pl.BlockSpec((tm, tn), lambda i, j, k: (i, j)),
        out_shape=jax.ShapeDtypeStruct((m, n), out_dtype),
        scratch_shapes=scratch,
        compiler_params=_cparams(("parallel", "parallel", "arbitrary")),
        name=name,
    )(a, w)


def _mm_ws_kernel(a_ref, w_ref, o_ref, wb_scr, *, act):
    @pl.when(pl.program_id(1) == 0)
    def _():
        wb_scr[...] = w_ref[0].astype(wb_scr.dtype)

    acc = _dg(a_ref[...], wb_scr[...])
    if act == "relu2":
        acc = jnp.square(jnp.maximum(acc, 0.0))
    o_ref[...] = acc.astype(o_ref.dtype)


def _mm_ws(a, w, layer, out_dtype=F32, act=None, name="mm_ws"):
    m, kd = a.shape
    n = w.shape[2]
    tm = _row_tile(m, (1040, 1024, 512, 256, 128, 64, 8))
    tn = _pick(n, (1024, 512, 256, 128))
    return pl.pallas_call(
        functools.partial(_mm_ws_kernel, act=act),
        grid=(n // tn, m // tm),
        in_specs=[pl.BlockSpec((tm, kd), lambda j, i: (i, 0)),
                  pl.BlockSpec((1, kd, tn), lambda j, i: (layer, 0, j))],
        out_specs=pl.BlockSpec((tm, tn), lambda j, i: (i, j)),
        out_shape=jax.ShapeDtypeStruct((m, n), out_dtype),
        scratch_shapes=[pltpu.VMEM((kd, tn), BF16)],
        compiler_params=_cparams(("arbitrary", "arbitrary")),
        name=name,
    )(a, w)


def _mm_resnorm(a, w, layer, res, g_post, g_next, name="mm_resnorm"):
    m, kd = a.shape
    n = w.shape[2]
    tm = _row_tile(m, (520, 512, 256, 128, 64, 8))
    tk = _pick(kd, (2048, 1024, 512))
    res_spec = pl.BlockSpec((tm, n), lambda i, k: (i, 0))
    nk = kd // tk
    scratch = [pltpu.VMEM((tm, n), F32)] if nk > 1 else []
    return pl.pallas_call(
        functools.partial(_mm_resnorm_kernel, nk=nk),
        grid=(m // tm, nk),
        in_specs=[pl.BlockSpec((tm, tk), lambda i, k: (i, k)),
                  pl.BlockSpec((1, tk, n), lambda i, k: (layer, k, 0)),
                  res_spec,
                  pl.BlockSpec((1, n), lambda i, k: (0, 0)),
                  pl.BlockSpec((1, n), lambda i, k: (0, 0))],
        out_specs=[pl.BlockSpec((tm, n), lambda i, k: (i, 0)),
                   pl.BlockSpec((tm, n), lambda i, k: (i, 0))],
        out_shape=[jax.ShapeDtypeStruct((m, n), F32), jax.ShapeDtypeStruct((m, n), BF16)],
        scratch_shapes=scratch,
        compiler_params=_cparams(("parallel", "arbitrary")),
        name=name,
    )(a, w, res, g_post.reshape(1, n), g_next.reshape(1, n))


def _merge_kernel(y_ref, w_ref, g_ref, o_ref, acc_ref):
    n = pl.program_id(2)
    z = _dg(y_ref[0], w_ref[0, 0]) * g_ref[...].astype(F32)

    @pl.when(n == 0)
    def _():
        acc_ref[...] = z

    @pl.when(n > 0)
    def _():
        acc_ref[...] += z

    @pl.when(n == N_BRANCH - 1)
    def _():
        o_ref[...] = acc_ref[...].astype(o_ref.dtype)


def _merge(y3, wb, layer, gate):
    _, m, bw = y3.shape
    d = wb.shape[3]
    tm = _row_tile(m, (1040, 1024, 512, 256, 128, 64, 8))
    tn = 1024
    nj = d // tn
    return pl.pallas_call(
        _merge_kernel,
        grid=(m // tm, nj, N_BRANCH),
        in_specs=[pl.BlockSpec((1, tm, bw), lambda i, j, n: (n, i, 0)),
                  pl.BlockSpec((1, 1, bw, tn), lambda i, j, n: (layer, n, 0, j)),
                  pl.BlockSpec((tm, tn), lambda i, j, n: (i, n * nj + j))],
        out_specs=pl.BlockSpec((tm, tn), lambda i, j, n: (i, j)),
        out_shape=jax.ShapeDtypeStruct((m, d), BF16),
        scratch_shapes=[pltpu.VMEM((tm, tn), F32)],
        compiler_params=_cparams(("parallel", "parallel", "arbitrary")),
        name="merge",
    )(y3, wb, gate)


def _cast_kernel(w_ref, o_ref):
    o_ref[...] = w_ref[...].astype(o_ref.dtype)


def _cast_bf16(w):
    g, r, c = w.shape
    tr = _pick(r, (256, 128, 64, 8))
    return pl.pallas_call(
        _cast_kernel,
        grid=(g, r // tr),
        in_specs=[pl.BlockSpec((1, tr, c), lambda a, i: (a, i, 0))],
        out_specs=pl.BlockSpec((1, tr, c), lambda a, i: (a, i, 0)),
        out_shape=jax.ShapeDtypeStruct((g, r, c), BF16),
        compiler_params=_cparams(("parallel", "parallel")),
        name="cast_bf16",
    )(w)


_SEGS = ((0, RW_PROJ, RW_PAD),
         (RW_PROJ, SSD_PROJ, SSD_PAD),
         (RW_PROJ + SSD_PROJ, GLA_PROJ, GLA_PAD),
         (RW_PROJ + SSD_PROJ + GLA_PROJ, N_BRANCH * D_MODEL, N_BRANCH * D_MODEL))


def _split_t_kernel(w_ref, o_ref, *, live, tc):
    x = jnp.transpose(w_ref[0])
    col = pl.program_id(1) * tc + lax.broadcasted_iota(jnp.int32, x.shape, 1)
    o_ref[0] = jnp.where(col < live, x, 0.0).astype(o_ref.dtype)


def _split_w_in_t(w_in_t):
    depth, _, d = w_in_t.shape
    outs = []
    for first, live, padded in _SEGS:
        tc = _pick(padded, (1152, 896, 640, 1024))
        outs.append(pl.pallas_call(
            functools.partial(_split_t_kernel, live=live, tc=tc),
            grid=(depth, padded // tc),
            in_specs=[pl.BlockSpec((pl.Element(1), pl.Element(tc), pl.Element(d)),
                                   lambda l, j, f=first, t=tc: (l, pl.multiple_of(f + j * t, 16), 0))],
            out_specs=pl.BlockSpec((1, d, tc), lambda l, j: (l, 0, j)),
            out_shape=jax.ShapeDtypeStruct((depth, d, padded), BF16),
            compiler_params=_cparams(("parallel", "parallel")),
            name="split_w_in_t",
        )(w_in_t))
    return outs


def _attn_p_kernel(q_ref, k_ref, v_ref, o_ref):
    s = _dgb(q_ref[...], k_ref[0], NT) * (X_HEAD ** -0.5)
    p = jnp.exp(s - jnp.max(s, axis=-1, keepdims=True))
    attn = p / jnp.sum(p, axis=-1, keepdims=True)
    o_ref[...] = _dgb(attn, v_ref[0]).astype(o_ref.dtype)


def _attn_prompt(q, mem_k, mem_v, nb, t):
    tq = _pick(t, (1024, 512, 256, 128, 64))
    nq = t // tq
    return pl.pallas_call(
        _attn_p_kernel,
        grid=(nb, X_HEADS, nq),
        in_specs=[pl.BlockSpec((tq, X_HEAD), lambda b, h, i: (b * nq + i, h)),
                  pl.BlockSpec((1, N_MEM, X_HEAD), lambda b, h, i: (b, 0, h)),
                  pl.BlockSpec((1, N_MEM, X_HEAD), lambda b, h, i: (b, 0, h))],
        out_specs=pl.BlockSpec((tq, X_HEAD), lambda b, h, i: (b * nq + i, h)),
        out_shape=jax.ShapeDtypeStruct(q.shape, BF16),
        compiler_params=_cparams(("parallel", "parallel", "parallel")),
        name="attn_prompt",
    )(q, mem_k, mem_v)


def _attn_s_kernel(q_ref, k_ref, v_ref, o_ref):
    q = q_ref[0].astype(F32)
    s = jnp.sum(k_ref[0, 0] * q[None], axis=-1, keepdims=True) * (X_HEAD ** -0.5)
    p = jnp.exp(s - jnp.max(s, axis=0, keepdims=True))
    attn = p / jnp.sum(p, axis=0, keepdims=True)
    o_ref[0] = jnp.sum(attn * v_ref[0, 0], axis=0).astype(o_ref.dtype)


def _attn_sample(q3, cache_k, cache_v, layer):
    nb = q3.shape[0]
    kv_spec = pl.BlockSpec((1, 1, N_MEM, X_HEADS, X_HEAD), lambda b: (layer, b, 0, 0, 0))
    return pl.pallas_call(
        _attn_s_kernel,
        grid=(nb,),
        in_specs=[pl.BlockSpec((1, X_HEADS, X_HEAD), lambda b: (b, 0, 0)), kv_spec, kv_spec],
        out_specs=pl.BlockSpec((1, X_HEADS, X_HEAD), lambda b: (b, 0, 0)),
        out_shape=jax.ShapeDtypeStruct((nb, X_HEADS, X_HEAD), BF16),
        compiler_params=_cparams(("parallel",)),
        name="attn_sample",
    )(q3, cache_k, cache_v)


def _rwkv_tokens(p, prev, mu, w0, w2, a0, a2, g2, k_k, k_a):
    c = BRANCH_W
    ps = p + (prev - p) * mu
    r = ps[:, 0:c]
    k = ps[:, c:2 * c]
    v = ps[:, 2 * c:3 * c]
    slab = ps[:, 3 * c:3 * c + 128]
    gslab = ps[:, 3 * c + 128:RW_PAD]
    wl = w0 + _dgb(jnp.tanh(slab), w2)
    lw = -jnp.exp(-_softplus(-wl) - 0.5)
    a = _sigmoid(a0 + _dgb(slab, a2))
    g = _dgb(_sigmoid(gslab), g2)
    kkraw = k * k_k
    k2 = k * (1.0 + (a - 1.0) * k_a)
    return r, k2, v, lw, a, g, kkraw


def _rwkv_p_kernel(p_ref, mu_ref, w0_ref, w2_ref, a0_ref, a2_ref, g2_ref, kk_ref, ka_ref, rk_ref,
                   lnw_ref, lnb_ref, y_ref, s_ref, prev_scr):
    cidx = pl.program_id(1)

    @pl.when(cidx == 0)
    def _():
        s_ref[...] = jnp.zeros_like(s_ref)
        prev_scr[...] = jnp.zeros_like(prev_scr)

    p = p_ref[...]
    n = p.shape[0]
    row1 = lax.broadcasted_iota(jnp.int32, (n, 1), 0)
    prev = jnp.where(row1 == 0, prev_scr[0:1, :], pltpu.roll(p, 1, 0))
    prev_scr[0:1, :] = p[n - 1:n, :]
    r, k2, v, lw, a, g, kkraw = _rwkv_tokens(p, prev, mu_ref[...], w0_ref[...], w2_ref[...], a0_ref[...],
                                             a2_ref[...], g2_ref[...], kk_ref[...], ka_ref[...])
    cum = _dot_sel_l(_tri_incl(n), lw)
    e_c = jnp.exp(cum)
    e_x = jnp.exp(cum - lw)
    e_n = jnp.exp(-cum)
    e_l = jnp.exp(cum[n - 1:n, :] - cum)
    rk = rk_ref[...]
    lnw = lnw_ref[...]
    lnb = lnb_ref[...]

    m0, row, col = _pair_masks(n)
    strict = col < row
    incl = col <= row
    eye = jnp.where(col == row, 1.0, 0.0)
    ms, _, _ = _pair_masks(RW_HEAD)

    prs = range(RW_HEADS // 2)
    sls = [slice(pi * LANES, (pi + 1) * LANES) for pi in prs]
    kkn = []
    for sl in sls:
        kkp = kkraw[:, sl]
        kkn.append(kkp / jnp.maximum(jnp.sqrt(_headsum_pair(kkp * kkp, m0)), 1e-12))
    bv = [kkn[i] * a[:, sls[i]] for i in prs]
    at = [-kkn[i] * e_x[:, sls[i]] for i in prs]
    rt = [r[:, sl] * e_c[:, sl] for sl in sls]
    vv = [v[:, sl] for sl in sls]
    ar = [jnp.concatenate([at[i], rt[i]], axis=0) for i in prs]
    sab = [_dot_hi(ar[i], _bd(bv[i] * e_n[:, sls[i]], m0), NT) for i in prs]
    sak = [_dgb(ar[i], _bd(k2[:, sls[i]] * e_n[:, sls[i]], m0), NT) for i in prs]
    a_ab = [jnp.where(strict, s[:n], 0.0) for s in sab]
    a_ak = [jnp.where(strict, s[:n], 0.0) for s in sak]
    m_rb = [jnp.where(incl, s[n:], 0.0) for s in sab]
    m_rk = [jnp.where(incl, s[n:], 0.0) for s in sak]
    tinv = [eye + x for x in a_ab]
    xs = a_ab
    akv = [_dgb(a_ak[i], _bd(vv[i], m0)) for i in prs]
    for _ in range(n.bit_length() - 2):
        xs = [_dot_hi(x, _bd(x, m0)) for x in xs]
        tinv = [tinv[i] + _dot_hi(tinv[i], _bd(xs[i], m0)) for i in prs]
    tw = [_dgb(tinv[i], jnp.concatenate([_bd(at[i], m0), _bd(akv[i], m0)], axis=1)) for i in prs]
    s0 = [s_ref[0, pi] for pi in prs]
    us = [_dgb(jnp.concatenate([tw[i][:, :LANES], rt[i]], axis=0), _bd(s0[i], ms), NT) for i in prs]
    u = [us[i][:n] + tw[i][:, LANES:] for i in prs]
    uv = [jnp.concatenate([u[i], vv[i]], axis=0) for i in prs]
    y = [us[i][n:] + _dgb(jnp.concatenate([m_rb[i], m_rk[i]], axis=1),
                             jnp.concatenate([_bd(u[i], m0), _bd(vv[i], m0)], axis=0)) for i in prs]
    for i in prs:
        sl = sls[i]
        bk = jnp.concatenate([bv[i] * e_l[:, sl], k2[:, sl] * e_l[:, sl]], axis=0)
        z = _dgb(uv[i], bk, TN)
        s_ref[0, i] = s0[i] * e_c[n - 1:n, sl] + jnp.where(ms, z[:RW_HEAD], z[RW_HEAD:])
    for i in prs:
        sl = sls[i]
        mean = _headsum_pair(y[i], m0) * (1.0 / RW_HEAD)
        d = y[i] - mean
        var = _headsum_pair(d * d, m0) * (1.0 / RW_HEAD)
        yn = d * lax.rsqrt(var + RW_GN_EPS) * lnw[:, sl] + lnb[:, sl]
        bonus = _headsum_pair(r[:, sl] * k2[:, sl] * rk[:, sl], m0) * vv[i]
        y_ref[0, :, sl] = ((yn + bonus) * g[:, sl]).astype(y_ref.dtype)


def _rwkv_weights(w2, a2, g2):
    lw = w2.shape[0]
    w2p = jnp.concatenate([w2, jnp.zeros((LANES - lw, BRANCH_W), F32)], axis=0).astype(BF16)
    a2p = jnp.concatenate([jnp.zeros((lw, BRANCH_W), F32), a2], axis=0).astype(BF16)
    g2p = jnp.pad(g2, ((0, RW_PAD - 3 * BRANCH_W - LANES - g2.shape[0]), (0, 0))).astype(BF16)
    return w2p, a2p, g2p


def _row(x):
    return x.reshape(1, -1)


def _const_spec(shape):
    nd = len(shape)
    return pl.BlockSpec(shape, lambda *_: (0,) * nd)


def _chained_call(kernel_fn, prev, inputs, in_specs, **kw):
    if prev is None:
        return pl.pallas_call(kernel_fn, in_specs=in_specs, **kw)(*inputs)
    n_in = len(inputs)

    def body(*refs):
        return kernel_fn(*refs[:n_in], *refs[n_in + 1:])

    return pl.pallas_call(body, in_specs=list(in_specs) + [pl.BlockSpec(memory_space=pl.ANY)],
                          input_output_aliases={n_in: 0}, **kw)(*inputs, prev)


def _put_kernel(rows_ref, o_ref):
    o_ref[...] = rows_ref[...].astype(o_ref.dtype)


def _put_rows(buf, rows, blk):
    g, n, c = rows.shape
    return _chained_call(
        _put_kernel, buf, (rows,), [_const_spec(rows.shape)],
        grid=(1,),
        out_specs=pl.BlockSpec((g, n, c), lambda i: (0, blk, 0)),
        out_shape=jax.ShapeDtypeStruct(buf.shape, buf.dtype),
        compiler_params=_cparams(("arbitrary",)),
        name="put_rows",
    )


def _rwkv_prompt(proj, nb, t, wts, y3, rows):
    nc = t // CHUNK
    consts = wts
    in_specs = [pl.BlockSpec((CHUNK, RW_PAD), lambda b, c: (b * nc + c, 0))]
    in_specs += [_const_spec(x.shape) for x in consts]
    return _chained_call(
        _rwkv_p_kernel, y3, (proj, *consts), in_specs,
        grid=(nb, nc),
        out_specs=[pl.BlockSpec((1, CHUNK, BRANCH_W), lambda b, c: (0, b * nc + c, 0)),
                   pl.BlockSpec((1, RW_HEADS // 2, RW_HEAD, LANES), lambda b, c: (b, 0, 0, 0))],
        out_shape=[jax.ShapeDtypeStruct((N_BRANCH, rows, BRANCH_W), BF16),
                   jax.ShapeDtypeStruct((nb, RW_HEADS // 2, RW_HEAD, LANES), F32)],
        scratch_shapes=[pltpu.VMEM((8, RW_PAD), F32)],
        compiler_params=_cparams(("parallel", "arbitrary")),
        name="rwkv_prompt",
    )


def _unpack_rwkv_state(sp):
    nb = sp.shape[0]
    s = sp.reshape(nb, RW_HEADS // 2, RW_HEAD, 2, RW_HEAD)
    return jnp.transpose(s, (0, 1, 3, 2, 4)).reshape(nb, RW_HEADS, RW_HEAD, RW_HEAD)


def _rwkv_s_prep_kernel(p_ref, prev_ref, mu_ref, w0_ref, w2_ref, a0_ref, a2_ref, g2_ref, kk_ref, ka_ref, rk_ref,
                        o_ref):
    r, k2, v, lw, a, g, kkraw = _rwkv_tokens(p_ref[...], prev_ref[...], mu_ref[...], w0_ref[...], w2_ref[...],
                                             a0_ref[...], a2_ref[...], g2_ref[...], kk_ref[...], ka_ref[...])
    n = r.shape[0]
    m0, _, _ = _pair_masks(n)
    rk = rk_ref[...]
    o_ref[0] = r
    o_ref[1] = jnp.exp(lw)
    o_ref[2] = k2
    o_ref[3] = v
    o_ref[6] = g
    for pi in range(RW_HEADS // 2):
        sl = slice(pi * LANES, (pi + 1) * LANES)
        kkp = kkraw[:, sl]
        kkn = kkp / jnp.maximum(jnp.sqrt(_headsum_pair(kkp * kkp, m0)), 1e-12)
        o_ref[4, :, sl] = -kkn
        o_ref[5, :, sl] = kkn * a[:, sl]
        o_ref[7, :, sl] = _headsum_pair(r[:, sl] * k2[:, sl] * rk[:, sl], m0) * v[:, sl]


def _rwkv_s_kernel(vec_ref, s_ref, lnw_ref, lnb_ref, so_ref, y_ref, y_scr):
    r = vec_ref[0, 0]
    w = vec_ref[1, 0]
    k = vec_ref[2, 0]
    a = vec_ref[4, 0]
    b = vec_ref[5, 0]

    def body(i, carry):
        s = s_ref[0, 0, i]
        sa = jnp.sum(s * a, axis=0, keepdims=True)
        sn = s * w + sa * b + vec_ref[3, 0, pl.ds(i, 1), :] * k
        so_ref[0, 0, i] = sn
        y_scr[pl.ds(i, 1), :] = jnp.sum(sn * r, axis=0, keepdims=True)
        return carry

    lax.fori_loop(0, RW_HEAD, body, 0)
    y = y_scr[...]
    d = y - jnp.mean(y, axis=0, keepdims=True)
    var = jnp.mean(d * d, axis=0, keepdims=True)
    yn = d * lax.rsqrt(var + RW_GN_EPS) * lnw_ref[0] + lnb_ref[0]
    y_ref[0] = (yn + vec_ref[7, 0]) * vec_ref[6, 0]


def _rwkv_sample(proj, row0_blk, shift_prev, state_t, layer, wts, lnw, lnb, s_all):
    nb = shift_prev.shape[0]
    consts = wts
    vec = pl.pallas_call(
        _rwkv_s_prep_kernel,
        grid=(1,),
        in_specs=[pl.BlockSpec((nb, RW_PAD), lambda i: (row0_blk, 0)),
                  pl.BlockSpec((nb, RW_PAD), lambda i: (0, 0))] + [_const_spec(x.shape) for x in consts],
        out_specs=pl.BlockSpec((8, nb, BRANCH_W), lambda i: (0, 0, 0)),
        out_shape=jax.ShapeDtypeStruct((8, nb, BRANCH_W), F32),
        compiler_params=_cparams(("arbitrary",)),
        name="rwkv_sample_prep",
    )(proj, shift_prev, *consts)
    vec_t = jnp.transpose(vec, (0, 2, 1)).reshape(8, RW_HEADS, RW_HEAD, nb)
    ln_shape = (RW_HEADS, RW_HEAD, nb)
    lnw_t = jnp.broadcast_to(lnw.reshape(RW_HEADS, RW_HEAD, 1), ln_shape)
    lnb_t = jnp.broadcast_to(lnb.reshape(RW_HEADS, RW_HEAD, 1), ln_shape)
    st_spec = pl.BlockSpec((1, 1, RW_HEAD, RW_HEAD, nb), lambda h: (layer, h, 0, 0, 0))
    ch_spec = pl.BlockSpec((1, RW_HEAD, nb), lambda h: (h, 0, 0))
    s_all, y_t = _chained_call(
        _rwkv_s_kernel, s_all, (vec_t, state_t, lnw_t, lnb_t),
        [pl.BlockSpec((8, 1, RW_HEAD, nb), lambda h: (0, h, 0, 0)), st_spec, ch_spec, ch_spec],
        grid=(RW_HEADS,),
        out_specs=[st_spec, ch_spec],
        out_shape=[jax.ShapeDtypeStruct(state_t.shape, F32), jax.ShapeDtypeStruct(ln_shape, F32)],
        scratch_shapes=[pltpu.VMEM((RW_HEAD, nb), F32)],
        compiler_params=_cparams(("parallel",)),
        name="rwkv_sample",
    )
    return jnp.transpose(y_t.reshape(BRANCH_W, nb)), s_all


def _expand_heads():
    k = lax.broadcasted_iota(jnp.int32, (LANES, BRANCH_W), 0)
    c = lax.broadcasted_iota(jnp.int32, (LANES, BRANCH_W), 1)
    return jnp.where(jnp.right_shift(c, 6) == k, 1.0, 0.0).astype(BF16)


def _ssd_p_kernel(p_ref, cw_ref, cb_ref, dtb_ref, alog_ref, dsk_ref, nw_ref, y_ref, st_ref, buf_scr):
    cidx = pl.program_id(1)
    n = p_ref.shape[0]

    @pl.when(cidx == 0)
    def _():
        st_ref[...] = jnp.zeros_like(st_ref)
        buf_scr[0:8, :] = jnp.zeros((8, SSD_CONV_DIM), F32)

    z = p_ref[:, 0:BRANCH_W]
    buf_scr[8:8 + n, :] = p_ref[:, BRANCH_W:BRANCH_W + SSD_CONV_DIM]
    conv = cb_ref[...]
    for i in range(SSD_CONV):
        conv = conv + cw_ref[i:i + 1, :] * buf_scr[pl.ds(8 - (SSD_CONV - 1) + i, n), :]
    buf_scr[0:8, :] = buf_scr[n:n + 8, :]
    xa = _silu(conv)
    xs = xa[:, 0:BRANCH_W]
    bm = xa[:, BRANCH_W:BRANCH_W + SSD_GROUPS * SSD_STATE]
    cm = xa[:, BRANCH_W + SSD_GROUPS * SSD_STATE:]
    dt = _softplus(p_ref[:, BRANCH_W + SSD_CONV_DIM:SSD_PAD] + dtb_ref[...])
    dte = _dot_sel_r(dt, _expand_heads())
    da = dte * (-jnp.exp(alog_ref[...]))
    cum = _dot_sel_l(_tri_incl(n), da)
    xdt = xs * dte
    ecum = jnp.exp(cum)
    cl = cum[n - 1:n, :]
    xdl = xdt * jnp.exp(cl - cum)
    pl_ = jnp.exp(cl)

    m0, row, col = _pair_masks(n)
    incl = col <= row
    eye = col == row
    gw = BRANCH_W // SSD_GROUPS
    ppg = gw // LANES
    ys = []
    for gi in range(SSD_GROUPS):
        gs = slice(gi * gw, (gi + 1) * gw)
        bg = bm[:, gi * SSD_STATE:(gi + 1) * SSD_STATE]
        cg = cm[:, gi * SSD_STATE:(gi + 1) * SSD_STATE]
        cbp = _dgb(cg, jnp.concatenate([bg, bg], axis=0), NT)
        st = st_ref[0, :, gs]
        cs = _dgb(cg, st)
        for q in range(ppg):
            sl = slice(gi * gw + q * LANES, gi * gw + (q + 1) * LANES)
            cp = cum[:, sl]
            rp = jnp.sum(jnp.where(eye, cp, 0.0), axis=0, keepdims=True)
            seg = jnp.exp(jnp.where(incl, cp - rp, -jnp.inf))
            yp = _dgb(cbp * seg, _bd(xdt[:, sl], m0)) + ecum[:, sl] * cs[:, q * LANES:(q + 1) * LANES]
            ys.append(yp)
        st_ref[0, :, gs] = st * pl_[:, gs] + _dgb(bg, xdl[:, gs], TN)
    y = jnp.concatenate(ys, axis=1) + xs * dsk_ref[...]
    y = y * _silu(z)
    for gi in range(SSD_GROUPS):
        gs = slice(gi * gw, (gi + 1) * gw)
        yg = y[:, gs]
        yg = yg * lax.rsqrt(jnp.mean(yg * yg, axis=-1, keepdims=True) + GROUP_EPS)
        y_ref[0, :, gs] = (yg * nw_ref[:, gs]).astype(y_ref.dtype)


def _ssd_consts(conv_w, conv_b, dt_bias, a_log, d_skip, norm_w):
    dtb = jnp.pad(dt_bias, (0, LANES - SSD_HEADS)).reshape(1, LANES)
    return (conv_w, _row(conv_b), dtb, _row(jnp.repeat(a_log, SSD_HEAD)), _row(jnp.repeat(d_skip, SSD_HEAD)),
            _row(norm_w))


def _ssd_prompt(proj, nb, t, consts, y3):
    nc = t // CHUNK
    return _chained_call(
        _ssd_p_kernel, y3, (proj, *consts),
        [pl.BlockSpec((CHUNK, SSD_PAD), lambda b, c: (b * nc + c, 0))] + [_const_spec(x.shape) for x in consts],
        grid=(nb, nc),
        out_specs=[pl.BlockSpec((1, CHUNK, BRANCH_W), lambda b, c: (1, b * nc + c, 0)),
                   pl.BlockSpec((1, SSD_STATE, BRANCH_W), lambda b, c: (b, 0, 0))],
        out_shape=[jax.ShapeDtypeStruct(y3.shape, BF16),
                   jax.ShapeDtypeStruct((nb, SSD_STATE, BRANCH_W), F32)],
        scratch_shapes=[pltpu.VMEM((CHUNK + 8, SSD_CONV_DIM), F32)],
        compiler_params=_cparams(("parallel", "arbitrary")),
        name="ssd_prompt",
    )


def _unpack_ssd_state(st):
    nb = st.shape[0]
    return jnp.transpose(st.reshape(nb, SSD_STATE, SSD_HEADS, SSD_HEAD), (0, 2, 3, 1))


def _ssd_s_prep_kernel(p_ref, cv_ref, cw_ref, cb_ref, dtb_ref, alog_ref, dsk_ref, o_ref, bc_ref):
    z = p_ref[:, 0:BRANCH_W]
    conv = cb_ref[...] + cw_ref[SSD_CONV - 1:SSD_CONV, :] * p_ref[:, BRANCH_W:BRANCH_W + SSD_CONV_DIM]
    for i in range(SSD_CONV - 1):
        conv = conv + cw_ref[i:i + 1, :] * cv_ref[i]
    xa = _silu(conv)
    xs = xa[:, 0:BRANCH_W]
    dt = _softplus(p_ref[:, BRANCH_W + SSD_CONV_DIM:SSD_PAD] + dtb_ref[...])
    dte = _dot_sel_r(dt, _expand_heads())
    o_ref[0] = xs * dte
    o_ref[1] = jnp.exp(dte * (-jnp.exp(alog_ref[...])))
    o_ref[2] = xs * dsk_ref[...]
    o_ref[3] = _silu(z)
    bc_ref[...] = xa[:, BRANCH_W:]


def _ssd_s_kernel(vec_ref, bc_ref, s_ref, nw_ref, so_ref, y_ref):
    nbb = s_ref.shape[1]
    n = SSD_HEAD
    eye = lax.broadcasted_iota(jnp.int32, (n, n), 0) == lax.broadcasted_iota(jnp.int32, (n, n), 1)
    hpg = SSD_HEADS // SSD_GROUPS

    def body(bi, carry):
        for h in range(SSD_HEADS):
            hs = pl.ds(h, 1)
            gi = h // hpg
            s = s_ref[0, bi, h]
            xdt_ = vec_ref[0, bi, hs, :]
            dec = vec_ref[1, bi, hs, 0:1]
            b_ = bc_ref[bi, pl.ds(gi, 1), :]
            c_ = bc_ref[bi, pl.ds(SSD_GROUPS + gi, 1), :]
            xc = jnp.sum(jnp.where(eye, xdt_, 0.0), axis=-1, keepdims=True)
            sn = s * dec + xc * b_
            so_ref[0, bi, h] = sn
            yc = jnp.sum(sn * c_, axis=-1, keepdims=True)
            yr = jnp.sum(jnp.where(eye, yc, 0.0), axis=0, keepdims=True)
            y_ref[bi, hs, :] = (yr + vec_ref[2, bi, hs, :]) * vec_ref[3, bi, hs, :]
        for gi in range(SSD_GROUPS):
            rs = pl.ds(gi * hpg, hpg)
            yg = y_ref[bi, rs, :]
            ms = jnp.sum(jnp.sum(yg * yg, axis=-1, keepdims=True), axis=0, keepdims=True) * (1.0 / (hpg * n))
            y_ref[bi, rs, :] = yg * lax.rsqrt(ms + GROUP_EPS) * nw_ref[rs, :]
        return carry

    lax.fori_loop(0, nbb, body, 0)


def _ssd_sample(proj, row0_blk, conv_prev, state, layer, consts, s_all):
    nb = conv_prev.shape[1]
    cw, cb, dtb, alog, dsk, nw = consts
    vec, bc = pl.pallas_call(
        _ssd_s_prep_kernel,
        grid=(1,),
        in_specs=[pl.BlockSpec((nb, SSD_PAD), lambda i: (row0_blk, 0)),
                  _const_spec(conv_prev.shape)] + [_const_spec(x.shape) for x in (cw, cb, dtb, alog, dsk)],
        out_specs=[_const_spec((4, nb, BRANCH_W)), _const_spec((nb, 2 * SSD_GROUPS * SSD_STATE))],
        out_shape=[jax.ShapeDtypeStruct((4, nb, BRANCH_W), F32),
                   jax.ShapeDtypeStruct((nb, 2 * SSD_GROUPS * SSD_STATE), F32)],
        compiler_params=_cparams(("arbitrary",)),
        name="ssd_sample_prep",
    )(proj, conv_prev, cw, cb, dtb, alog, dsk)
    vec = vec.reshape(4, nb, SSD_HEADS, SSD_HEAD)
    bc = bc.reshape(nb, 2 * SSD_GROUPS, SSD_STATE)
    bb = 8
    st_spec = pl.BlockSpec((1, bb, SSD_HEADS, SSD_HEAD, SSD_STATE), lambda i: (layer, i, 0, 0, 0))
    s_all, y = _chained_call(
        _ssd_s_kernel, s_all, (vec, bc, state, nw.reshape(SSD_HEADS, SSD_HEAD)),
        [pl.BlockSpec((4, bb, SSD_HEADS, SSD_HEAD), lambda i: (0, i, 0, 0)),
         pl.BlockSpec((bb, 2 * SSD_GROUPS, SSD_STATE), lambda i: (i, 0, 0)), st_spec,
         _const_spec((SSD_HEADS, SSD_HEAD))],
        grid=(nb // bb,),
        out_specs=[st_spec, pl.BlockSpec((bb, SSD_HEADS, SSD_HEAD), lambda i: (i, 0, 0))],
        out_shape=[jax.ShapeDtypeStruct(state.shape, F32),
                   jax.ShapeDtypeStruct((nb, SSD_HEADS, SSD_HEAD), F32)],
        compiler_params=_cparams(("parallel",)),
        name="ssd_sample",
    )
    return y.reshape(nb, BRANCH_W), s_all


def _gla_tokens(p, au, ab):
    q = p[:, 0:GLA_DK] * (GLA_HK ** -0.5)
    k = p[:, GLA_DK:2 * GLA_DK]
    v = p[:, 2 * GLA_DK:2 * GLA_DK + GLA_DV]
    r = p[:, 2 * GLA_DK + GLA_DV:2 * GLA_DK + 2 * GLA_DV]
    ad = p[:, 2 * GLA_DK + 2 * GLA_DV:GLA_PAD]
    lg = -_softplus(-(_dgb(ad, au) + ab)) * (1.0 / GLA_TAU)
    return q, k, v, r, lg


def _gla_p_kernel(p_ref, au_ref, ab_ref, nw_ref, o_ref, st_ref):
    cidx = pl.program_id(1)

    @pl.when(cidx == 0)
    def _():
        st_ref[...] = jnp.zeros_like(st_ref)

    n = p_ref.shape[0]
    q, k, v, r, lg = _gla_tokens(p_ref[...], au_ref[...], ab_ref[...])
    cum = _dot_sel_l(_tri_incl(n), lg)
    cumx = cum - lg
    cl = cum[n - 1:n, :]
    qe = q * jnp.exp(cum)
    kb = k * jnp.exp(cl - cum)
    pl_ = jnp.exp(cl)
    rowi = lax.broadcasted_iota(jnp.int32, (SUB, 1), 0)
    for h in range(GLA_HEADS):
        sk = slice(h * GLA_HK, (h + 1) * GLA_HK)
        sv = slice(h * GLA_HV, (h + 1) * GLA_HV)
        st = st_ref[0, h]
        vh = v[:, sv]
        parts = []
        for blk in range(n // SUB):
            lo = blk * SUB
            rs = slice(lo, lo + SUB)
            q_i = q[rs, sk]
            c_i = cum[rs, sk]
            k_i = k[rs, sk]
            v_i = vh[rs]
            if blk > 0:
                cref = cumx[lo:lo + 1, sk]
                qt = q_i * jnp.exp(c_i - cref)
                kt = k[0:lo, sk] * jnp.exp(cref - cum[0:lo, sk])
                acc = _dgb(_dgb(qt, kt, NT), vh[0:lo])
            else:
                acc = jnp.zeros((SUB, GLA_HV), F32)
            for s in range(SUB):
                e = jnp.exp(c_i - c_i[s:s + 1])
                w = jnp.sum(q_i * k_i[s:s + 1] * e, axis=-1, keepdims=True)
                acc = acc + jnp.where(rowi >= s, w, 0.0) * v_i[s:s + 1]
            parts.append(acc)
        o = jnp.concatenate(parts, axis=0) + _dgb(qe[:, sk], st, NT)
        st_ref[0, h] = st * pl_[:, sk] + _dgb(vh, kb[:, sk], TN)
        o = o * lax.rsqrt(jnp.mean(o * o, axis=-1, keepdims=True) + GROUP_EPS) * nw_ref[...]
        o_ref[0, :, sv] = (o * _silu(r[:, sv])).astype(o_ref.dtype)


def _gla_consts(alpha_up, alpha_b, norm_w):
    au = jnp.pad(alpha_up, ((0, LANES - GLA_LORA), (0, 0))).astype(BF16)
    return au, _row(alpha_b), _row(norm_w)


def _gla_prompt(proj, nb, t, consts, y3):
    nc = t // CHUNK
    return _chained_call(
        _gla_p_kernel, y3, (proj, *consts),
        [pl.BlockSpec((CHUNK, GLA_PAD), lambda b, c: (b * nc + c, 0))] + [_const_spec(x.shape) for x in consts],
        grid=(nb, nc),
        out_specs=[pl.BlockSpec((1, CHUNK, GLA_DV), lambda b, c: (2, b * nc + c, 0)),
                   pl.BlockSpec((1, GLA_HEADS, GLA_HV, GLA_HK), lambda b, c: (b, 0, 0, 0))],
        out_shape=[jax.ShapeDtypeStruct(y3.shape, BF16),
                   jax.ShapeDtypeStruct((nb, GLA_HEADS, GLA_HV, GLA_HK), F32)],
        compiler_params=_cparams(("parallel", "arbitrary")),
        name="gla_prompt",
    )


def _gla_s_prep_kernel(p_ref, au_ref, ab_ref, qk_ref, vr_ref):
    q, k, v, r, lg = _gla_tokens(p_ref[...], au_ref[...], ab_ref[...])
    qk_ref[0] = q
    qk_ref[1] = k
    qk_ref[2] = jnp.exp(lg)
    vr_ref[0] = v
    vr_ref[1] = _silu(r)


def _gla_s_kernel(qk_ref, vr_ref, s_ref, nw_ref, so_ref, y_ref):
    nbb = s_ref.shape[1]
    n = GLA_HK
    eye = lax.broadcasted_iota(jnp.int32, (n, n), 0) == lax.broadcasted_iota(jnp.int32, (n, n), 1)

    def col(x):
        return jnp.sum(jnp.where(eye, x, 0.0), axis=-1, keepdims=True)

    def body(bi, carry):
        for h in range(GLA_HEADS):
            hs = pl.ds(h, 1)
            s = s_ref[0, bi, h]
            sn = s * col(qk_ref[2, bi, hs, :]) + col(qk_ref[1, bi, hs, :]) * vr_ref[0, bi, hs, :]
            so_ref[0, bi, h] = sn
            o = jnp.sum(sn * col(qk_ref[0, bi, hs, :]), axis=0, keepdims=True)
            o = o * lax.rsqrt(jnp.mean(o * o, axis=-1, keepdims=True) + GROUP_EPS) * nw_ref[...]
            y_ref[bi, hs, :] = o * vr_ref[1, bi, hs, :]
        return carry

    lax.fori_loop(0, nbb, body, 0)


def _gla_sample(proj, row0_blk, nb, state, layer, consts, s_all):
    au, ab, nw = consts
    qk, vr = pl.pallas_call(
        _gla_s_prep_kernel,
        grid=(1,),
        in_specs=[pl.BlockSpec((nb, GLA_PAD), lambda i: (row0_blk, 0)), _const_spec(au.shape),
                  _const_spec(ab.shape)],
        out_specs=[_const_spec((3, nb, GLA_DK)), _const_spec((2, nb, GLA_DV))],
        out_shape=[jax.ShapeDtypeStruct((3, nb, GLA_DK), F32), jax.ShapeDtypeStruct((2, nb, GLA_DV), F32)],
        compiler_params=_cparams(("arbitrary",)),
        name="gla_sample_prep",
    )(proj, au, ab)
    qk = qk.reshape(3, nb, GLA_HEADS, GLA_HK)
    vr = vr.reshape(2, nb, GLA_HEADS, GLA_HV)
    bb = 8
    st_spec = pl.BlockSpec((1, bb, GLA_HEADS, GLA_HK, GLA_HV), lambda i: (layer, i, 0, 0, 0))
    s_all, y = _chained_call(
        _gla_s_kernel, s_all, (qk, vr, state, nw),
        [pl.BlockSpec((3, bb, GLA_HEADS, GLA_HK), lambda i: (0, i, 0, 0)),
         pl.BlockSpec((2, bb, GLA_HEADS, GLA_HV), lambda i: (0, i, 0, 0)), st_spec, _const_spec(nw.shape)],
        grid=(nb // bb,),
        out_specs=[st_spec, pl.BlockSpec((bb, GLA_HEADS, GLA_HV), lambda i: (i, 0, 0))],
        out_shape=[jax.ShapeDtypeStruct(state.shape, F32),
                   jax.ShapeDtypeStruct((nb, GLA_HEADS, GLA_HV), F32)],
        compiler_params=_cparams(("parallel",)),
        name="gla_sample",
    )
    return y.reshape(nb, GLA_DV), s_all


def kernel(x_prompt, x_sample, state_rwkv, state_rwkv_shift, state_ssd, state_ssd_conv, state_gla, cache_mem_k, cache_mem_v, mem_prompt, norm_mix_pre, norm_mix_post, norm_x_pre, norm_x_post, norm_ffn_pre, norm_ffn_post, w_in, rw_mu, rw_w0, rw_w2, rw_a0, rw_a2, rw_g2, rw_kk, rw_ka, rw_rk, rw_ln_w, rw_ln_b, ssd_conv_w, ssd_conv_b, ssd_dt_bias, ssd_a_log, ssd_d, ssd_norm_w, gla_alpha_up, gla_alpha_b, gla_norm_w, w_branch, w_out, x_mem_norm, x_wq, x_wk, x_wv, x_wo, ffn_up, ffn_down):
    nbp, t, d = x_prompt.shape
    nbs = x_sample.shape[0]
    depth = w_in.shape[0]
    mp = nbp * t
    m = mp + nbs
    sblk = mp // nbs

    x = jnp.concatenate([x_prompt.reshape(mp, d), x_sample.reshape(nbs, d)], axis=0)
    mem_rows = mem_prompt.reshape(nbp * N_MEM, d)
    tm_norm = _row_tile(m, (1040, 1024, 512, 256, 128, 64, 8))

    w_rw_all, w_ssd_all, w_gla_all, w_gate_all = _split_w_in_t(jnp.transpose(w_in, (0, 2, 1)))
    wb_all = _cast_bf16(w_branch.reshape(depth * N_BRANCH, BRANCH_W, d)).reshape(depth, N_BRANCH, BRANCH_W, d)
    w_out_b, wo_b, down_b = [_cast_bf16(w) for w in (w_out, x_wo, ffn_down)]

    xn = _norm_rows(x, norm_mix_pre[0], tm_norm)
    p_acc = [[] for _ in range(7)]
    s_small = [[] for _ in range(2)]
    rw_ss = ssd_ss = gla_ss = None
    state_rwkv_t = jnp.transpose(state_rwkv, (0, 2, 3, 4, 1))
    for l in range(depth):
        proj_rw = _mm(xn, w_rw_all, l, name="proj_rwkv")
        proj_ssd = _mm(xn, w_ssd_all, l, name="proj_ssd")
        proj_gla = _mm(xn, w_gla_all, l, name="proj_gla")
        gate = _mm(xn, w_gate_all, l, out_dtype=BF16, act="sigmoid", name="proj_gate")

        w2p, a2p, g2p = _rwkv_weights(rw_w2[l], rw_a2[l], rw_g2[l])
        rw_tok = (_row(jnp.pad(rw_mu[l], (0, RW_PAD - RW_PROJ))), _row(rw_w0[l]), w2p, _row(rw_a0[l]), a2p, g2p,
                  _row(rw_kk[l]), _row(rw_ka[l]), _row(rw_rk[l]))
        y3, rw_sp = _rwkv_prompt(proj_rw, nbp, t, rw_tok + (_row(rw_ln_w[l]), _row(rw_ln_b[l])), None, m)
        shift_prev = jnp.pad(state_rwkv_shift[l], ((0, 0), (0, RW_PAD - RW_PROJ)))
        ya_s, rw_ss = _rwkv_sample(proj_rw, sblk, shift_prev, state_rwkv_t, l, rw_tok, rw_ln_w[l], rw_ln_b[l], rw_ss)

        ssd_c = _ssd_consts(ssd_conv_w[l], ssd_conv_b[l], ssd_dt_bias[l], ssd_a_log[l], ssd_d[l], ssd_norm_w[l])
        y3, ssd_sp = _ssd_prompt(proj_ssd, nbp, t, ssd_c, y3)
        conv_prev = jnp.transpose(state_ssd_conv[l], (1, 0, 2))
        yb_s, ssd_ss = _ssd_sample(proj_ssd, sblk, conv_prev, state_ssd, l, ssd_c, ssd_ss)

        gla_c = _gla_consts(gla_alpha_up[l], gla_alpha_b[l], gla_norm_w[l])
        y3, gla_sp = _gla_prompt(proj_gla, nbp, t, gla_c, y3)
        yc_s, gla_ss = _gla_sample(proj_gla, sblk, nbs, state_gla, l, gla_c, gla_ss)

        y3 = _put_rows(y3, jnp.stack([ya_s, yb_s, yc_s]), sblk)
        mix = _merge(y3, wb_all, l, gate)
        x, xn = _mm_resnorm(mix, w_out_b, l, x, norm_mix_post[l], norm_x_pre[l], name="out_proj")

        mn = _norm_rows(mem_rows, x_mem_norm[l], _row_tile(mem_rows.shape[0], (1024, 512, 256, 128, 64, 8)))
        mk = _mm_ws(mn, x_wk, l, name="mem_k")
        mv = _mm_ws(mn, x_wv, l, name="mem_v")
        q = _mm_ws(xn, x_wq, l, out_dtype=BF16, name="attn_q")
        o = _attn_prompt(q, mk.reshape(nbp, N_MEM, d), mv.reshape(nbp, N_MEM, d), nbp, t)
        o_s = _attn_sample(q[mp:].reshape(nbs, X_HEADS, X_HEAD), cache_mem_k, cache_mem_v, l)
        o = _put_rows(o.reshape(1, m, d), o_s.reshape(1, nbs, d), sblk).reshape(m, d)
        x, xn = _mm_resnorm(o, wo_b, l, x, norm_x_post[l], norm_ffn_pre[l], name="attn_out")

        hf = _mm_ws(xn, ffn_up, l, out_dtype=BF16, act="relu2", name="ffn_up")
        g_next = norm_mix_pre[l + 1] if l + 1 < depth else norm_mix_pre[l]
        x, xn = _mm_resnorm(hf, down_b, l, x, norm_ffn_post[l], g_next, name="ffn_down")

        last = [b * t + t - 1 for b in range(nbp)]
        rw_shift_p = jnp.stack([proj_rw[i, :RW_PROJ] for i in last])
        ssd_conv_p = jnp.stack([proj_ssd[i - (SSD_CONV - 2):i + 1, BRANCH_W:BRANCH_W + SSD_CONV_DIM] for i in last])
        new_p = (_unpack_rwkv_state(rw_sp), rw_shift_p, _unpack_ssd_state(ssd_sp), ssd_conv_p,
                 jnp.transpose(gla_sp, (0, 1, 3, 2)),
                 mk.reshape(nbp, N_MEM, X_HEADS, X_HEAD), mv.reshape(nbp, N_MEM, X_HEADS, X_HEAD))
        xbc_s = proj_ssd[mp:, BRANCH_W:BRANCH_W + SSD_CONV_DIM]
        new_s = (proj_rw[mp:, :RW_PROJ],
                 jnp.concatenate([state_ssd_conv[l][:, 1:], xbc_s[:, None, :]], axis=1))
        for acc, val in zip(p_acc, new_p):
            acc.append(val)
        for acc, val in zip(s_small, new_s):
            acc.append(val)

    outs_p = [jnp.stack(a) for a in p_acc]
    s_shift, s_conv = [jnp.stack(a) for a in s_small]
    return (x[:mp].reshape(nbp, t, d), x[mp:].reshape(nbs, 1, d), *outs_p,
            jnp.transpose(rw_ss, (0, 4, 1, 2, 3)), s_shift, ssd_ss, s_conv, gla_ss)
```
